```python
import math
import jax
import jax.numpy as jnp
from jax import lax
import numpy as np

D_MODEL = 4096
BATCH = 4
SEQ = 2048
DEPTH = 2

MEM_LEN = 256
HEAD_DIM = 128
BW = D_MODEL // 4
N_BRANCH = 4
A_HEADS = BW // HEAD_DIM
MOBA_BLOCK = 256
MOBA_TOPK = 3
MOBA_Q_BLOCK = 32
B_HEADS = BW // (2 * HEAD_DIM)
B_V_DIM = 2 * HEAD_DIM
ATTN_Q_BLOCK = 128
SGU_CHUNK = 128
SGU_GROUP_CH = 128
SGU_GROUPS = BW // SGU_GROUP_CH
M_HEADS = 4
M_HEAD_DIM = BW // M_HEADS
N_ALIBI_HEADS = A_HEADS + B_HEADS
RMS_EPS = 1e-6
LN_EPS = 1e-5
NEG = -1e30
W_IN_COLS = 13 * BW + N_BRANCH * D_MODEL
SPLIT_POINTS = tuple(BW * i for i in range(1, 14))

kernel_name = "hybrid_moba_diff_sgu_mem_gated"


def rmsnorm(x, g, eps=RMS_EPS):
    xf = x.astype(jnp.float32)
    y = xf * lax.rsqrt(jnp.mean(xf * xf, axis=-1, keepdims=True) + eps)
    return (y * g.astype(jnp.float32)).astype(x.dtype)


def layernorm(x, g, b, eps=LN_EPS):
    xf = x.astype(jnp.float32)
    mu = jnp.mean(xf, axis=-1, keepdims=True)
    var = jnp.mean(jnp.square(xf - mu), axis=-1, keepdims=True)
    y = (xf - mu) * lax.rsqrt(var + eps)
    return (y * g.astype(jnp.float32) + b.astype(jnp.float32)).astype(x.dtype)


def alibi_slopes(n):
    return jnp.exp2(-8.0 * jnp.arange(1, n + 1, dtype=jnp.float32) / n)


def moba_attention(q, k, v, slopes):
    bsz, seq, nh, hd = q.shape
    nb = -(-seq // MOBA_BLOCK)
    pad = nb * MOBA_BLOCK - seq
    padw = ((0, 0), (0, pad), (0, 0), (0, 0))
    kb = jnp.pad(k, padw).reshape(bsz, nb, MOBA_BLOCK, nh, hd).transpose(0, 3, 1, 2, 4)
    vb = jnp.pad(v, padw).reshape(bsz, nb, MOBA_BLOCK, nh, hd).transpose(0, 3, 1, 2, 4)
    k_mean = jnp.mean(kb, axis=3)
    q_blk = jnp.arange(seq) // MOBA_BLOCK
    gate = jnp.einsum('bshd,bhnd->bhsn', q, k_mean).astype(jnp.float32)
    past = jnp.arange(nb)[None, :] < q_blk[:, None]
    gate = jnp.where(past, gate, NEG)
    kk = max(min(MOBA_TOPK, nb - 1), 1)
    _, sel = lax.top_k(gate, kk)
    valid = sel < q_blk[None, None, :, None]

    nq = seq // MOBA_Q_BLOCK
    q_c = q.reshape(bsz, nq, MOBA_Q_BLOCK, nh, hd).transpose(1, 0, 2, 3, 4)
    sel_c = sel.reshape(bsz, nh, nq, MOBA_Q_BLOCK, kk).transpose(2, 0, 1, 3, 4)
    val_c = valid.reshape(bsz, nh, nq, MOBA_Q_BLOCK, kk).transpose(2, 0, 1, 3, 4)
    bi = jnp.arange(bsz)[:, None, None, None]
    hi = jnp.arange(nh)[None, :, None, None]
    offs = jnp.arange(MOBA_BLOCK)
    scale = hd ** -0.5

    def step(args):
        qi, qc, sc_idx, vmask = args
        t = qi * MOBA_Q_BLOCK + jnp.arange(MOBA_Q_BLOCK)
        own = (qi * MOBA_Q_BLOCK) // MOBA_BLOCK
        k_own = lax.dynamic_index_in_dim(kb, own, axis=2, keepdims=False)
        v_own = lax.dynamic_index_in_dim(vb, own, axis=2, keepdims=False)
        d_own = t[:, None] - (own * MOBA_BLOCK + offs)[None, :]
        s_own = (jnp.einsum('bqhd,bhpd->bhqp', qc, k_own).astype(jnp.float32) * scale
                 - slopes[None, :, None, None] * jnp.abs(d_own).astype(jnp.float32))
        s_own = jnp.where(d_own >= 0, s_own, NEG)
        k_sel = kb[bi, hi, sc_idx]
        v_sel = vb[bi, hi, sc_idx]
        d_sel = t[None, None, :, None, None] - (sc_idx[..., None] * MOBA_BLOCK + offs)
        s_sel = (jnp.einsum('bqhd,bhqjpd->bhqjp', qc, k_sel).astype(jnp.float32) * scale
                 - slopes[None, :, None, None, None] * jnp.abs(d_sel).astype(jnp.float32))
        s_sel = jnp.where(vmask[..., None], s_sel, NEG)
        s_all = jnp.concatenate([s_own, s_sel.reshape(bsz, nh, MOBA_Q_BLOCK, kk * MOBA_BLOCK)], axis=-1)
        p = jax.nn.softmax(s_all, axis=-1).astype(v.dtype)
        p_own = p[..., :MOBA_BLOCK]
        p_sel = p[..., MOBA_BLOCK:].reshape(bsz, nh, MOBA_Q_BLOCK, kk, MOBA_BLOCK)
        return (jnp.einsum('bhqp,bhpd->bqhd', p_own, v_own)
                + jnp.einsum('bhqjp,bhqjpd->bqhd', p_sel, v_sel))

    out = lax.map(step, (jnp.arange(nq), q_c, sel_c, val_c))
    return out.transpose(1, 0, 2, 3, 4).reshape(bsz, seq, nh, hd)


def diff_attention(q, k, v, lam, slopes):
    bsz, seq, nh, _, dq = q.shape
    dv = v.shape[-1]
    nq = seq // ATTN_Q_BLOCK
    q_c = q.reshape(bsz, nq, ATTN_Q_BLOCK, nh, 2, dq).transpose(1, 0, 2, 3, 4, 5)
    s_pos = jnp.arange(seq)
    scale = dq ** -0.5

    def step(args):
        qi, qc = args
        t = qi * ATTN_Q_BLOCK + jnp.arange(ATTN_Q_BLOCK)
        d = t[:, None] - s_pos[None, :]
        sc = (jnp.einsum('bqhmd,bshmd->bhmqs', qc, k).astype(jnp.float32) * scale
              - slopes[None, :, None, None, None] * jnp.abs(d).astype(jnp.float32))
        sc = jnp.where(d >= 0, sc, NEG)
        p = jax.nn.softmax(sc, axis=-1)
        w = (p[:, :, 0] - lam * p[:, :, 1]).astype(v.dtype)
        return jnp.einsum('bhqs,bshd->bqhd', w, v)

    out = lax.map(step, (jnp.arange(nq), q_c))
    return out.transpose(1, 0, 2, 3, 4).reshape(bsz, seq, nh, dv)


def spatial_gating(u, v, ln_g, ln_b, w_s, b_s):
    bsz, seq, c = u.shape
    nc = seq // SGU_CHUNK
    vn = layernorm(v, ln_g, ln_b).reshape(bsz, nc, SGU_CHUNK, SGU_GROUPS, SGU_GROUP_CH)
    causal = jnp.tril(jnp.ones((SGU_CHUNK, SGU_CHUNK), dtype=bool))
    w = jnp.where(causal[None], w_s, jnp.zeros_like(w_s))
    mixed = jnp.einsum('gts,bcsge->bctge', w, vn) + b_s.T[None, None, :, :, None]
    return u * mixed.reshape(bsz, seq, c)


def memory_attention(q, mem_k, mem_v):
    scale = q.shape[-1] ** -0.5
    sc = jnp.einsum('bshd,bmhd->bhsm', q, mem_k).astype(jnp.float32) * scale
    p = jax.nn.softmax(sc, axis=-1).astype(mem_v.dtype)
    return jnp.einsum('bhsm,bmhd->bshd', p, mem_v)


def hybrid_layer(x, mem, layer_idx, norm_g, w_in, mem_norm_g, w_mem_kv, lam_q1, lam_k1, lam_q2, lam_k2,
                 diff_subln_g, sgu_ln_g, sgu_ln_b, sgu_w, sgu_b, w_branch, w_out):
    bsz, seq, _ = x.shape
    h = rmsnorm(x, norm_g)
    proj = h @ w_in
    (qa, ka, va, za, qb, kb, vb, zb, uc, vc, zc, qm, zm, gate_logits) = jnp.split(proj, SPLIT_POINTS, axis=-1)
    slopes = alibi_slopes(N_ALIBI_HEADS)

    sa = (bsz, seq, A_HEADS, HEAD_DIM)
    ya = moba_attention(qa.reshape(sa), ka.reshape(sa), va.reshape(sa), slopes[B_HEADS:])
    ya = ya.reshape(bsz, seq, BW) * jax.nn.silu(za)

    lam_init = 0.8 - 0.6 * math.exp(-0.3 * layer_idx)
    lam = (jnp.exp(jnp.sum(lam_q1.astype(jnp.float32) * lam_k1.astype(jnp.float32)))
           - jnp.exp(jnp.sum(lam_q2.astype(jnp.float32) * lam_k2.astype(jnp.float32))) + lam_init)
    sb = (bsz, seq, B_HEADS, 2, HEAD_DIM)
    ob = diff_attention(qb.reshape(sb), kb.reshape(sb), vb.reshape(bsz, seq, B_HEADS, B_V_DIM), lam,
                        slopes[:B_HEADS])
    ob = rmsnorm(ob, diff_subln_g) * (1.0 - lam_init)
    yb = ob.reshape(bsz, seq, BW) * jax.nn.silu(zb)

    yc = spatial_gating(jax.nn.gelu(uc), jax.nn.gelu(vc), sgu_ln_g, sgu_ln_b, sgu_w, sgu_b) * jax.nn.silu(zc)

    mlen = mem.shape[1]
    mk, mv = jnp.split(rmsnorm(mem, mem_norm_g) @ w_mem_kv, 2, axis=-1)
    sm = (bsz, mlen, M_HEADS, M_HEAD_DIM)
    ym = memory_attention(qm.reshape(bsz, seq, M_HEADS, M_HEAD_DIM), mk.reshape(sm), mv.reshape(sm))
    ym = ym.reshape(bsz, seq, BW) * jax.nn.silu(zm)

    gates = jax.nn.sigmoid(gate_logits.reshape(bsz, seq, N_BRANCH, D_MODEL))
    merged = gates[:, :, 0] * (ya @ w_branch[0])
    merged = merged + gates[:, :, 1] * (yb @ w_branch[1])
    merged = merged + gates[:, :, 2] * (yc @ w_branch[2])
    merged = merged + gates[:, :, 3] * (ym @ w_branch[3])
    return x + merged @ w_out


def setup_inputs(seed: int = 0) -> dict:
    key = jax.random.key(seed)
    ks = jax.random.split(key, 20)
    f32 = jnp.float32
    nrm = lambda k, shape, s: jax.random.normal(k, shape, f32) * s
    return {
        "x": nrm(ks[0], (BATCH, SEQ, D_MODEL), 1.0),
        "mem": nrm(ks[1], (BATCH, MEM_LEN, D_MODEL), 1.0),
        "norm_g": 1.0 + nrm(ks[2], (DEPTH, D_MODEL), 0.02),
        "w_in": nrm(ks[3], (DEPTH, D_MODEL, W_IN_COLS), D_MODEL ** -0.5),
        "mem_norm_g": 1.0 + nrm(ks[4], (DEPTH, D_MODEL), 0.02),
        "w_mem_kv": nrm(ks[5], (DEPTH, D_MODEL, 2 * BW), D_MODEL ** -0.5),
        "diff_lam_q1": nrm(ks[6], (DEPTH, HEAD_DIM), 0.1),
        "diff_lam_k1": nrm(ks[7], (DEPTH, HEAD_DIM), 0.1),
        "diff_lam_q2": nrm(ks[8], (DEPTH, HEAD_DIM), 0.1),
        "diff_lam_k2": nrm(ks[9], (DEPTH, HEAD_DIM), 0.1),
        "diff_subln_g": 1.0 + nrm(ks[10], (DEPTH, B_V_DIM), 0.02),
        "sgu_ln_g": 1.0 + nrm(ks[11], (DEPTH, BW), 0.02),
        "sgu_ln_b": nrm(ks[12], (DEPTH, BW), 0.02),
        "sgu_w": nrm(ks[13], (DEPTH, SGU_GROUPS, SGU_CHUNK, SGU_CHUNK), SGU_CHUNK ** -0.5),
        "sgu_b": 1.0 + nrm(ks[14], (DEPTH, SGU_GROUPS, SGU_CHUNK), 0.02),
        "w_branch": nrm(ks[15], (DEPTH, N_BRANCH, BW, D_MODEL), BW ** -0.5),
        "w_out": nrm(ks[16], (DEPTH, D_MODEL, D_MODEL), D_MODEL ** -0.5),
        "final_g": 1.0 + nrm(ks[17], (D_MODEL,), 0.02),
    }


def reference(x, mem, norm_g, w_in, mem_norm_g, w_mem_kv, diff_lam_q1, diff_lam_k1, diff_lam_q2, diff_lam_k2,
              diff_subln_g, sgu_ln_g, sgu_ln_b, sgu_w, sgu_b, w_branch, w_out, final_g):
    for l in range(DEPTH):
        x = hybrid_layer(x, mem, l, norm_g[l], w_in[l], mem_norm_g[l], w_mem_kv[l],
                         diff_lam_q1[l], diff_lam_k1[l], diff_lam_q2[l], diff_lam_k2[l],
                         diff_subln_g[l], sgu_ln_g[l], sgu_ln_b[l], sgu_w[l], sgu_b[l],
                         w_branch[l], w_out[l])
    return rmsnorm(x, final_g)
```

```python
import functools
import math

import jax
import jax.numpy as jnp
from jax import lax
from jax.experimental import pallas as pl
from jax.experimental.pallas import tpu as pltpu

D_MODEL = 4096
BATCH = 4
SEQ = 2048
DEPTH = 2
MEM_LEN = 256
HEAD_DIM = 128
BW = D_MODEL // 4
N_BRANCH = 4
A_HEADS = BW // HEAD_DIM
MOBA_BLOCK = 256
MOBA_TOPK = 3
B_HEADS = BW // (2 * HEAD_DIM)
B_V_DIM = 2 * HEAD_DIM
SGU_CHUNK = 128
SGU_GROUP_CH = 128
SGU_GROUPS = BW // SGU_GROUP_CH
M_HEADS = 4
M_HEAD_DIM = BW // M_HEADS
N_ALIBI_HEADS = A_HEADS + B_HEADS
RMS_EPS = 1e-6
LN_EPS = 1e-5
NEG = -1e30
W_IN_COLS = 13 * BW + N_BRANCH * D_MODEL
N_MOBA_BLOCKS = SEQ // MOBA_BLOCK

(COL_QA, COL_KA, COL_VA, COL_ZA, COL_QB, COL_KB, COL_VB, COL_ZB,
 COL_UC, COL_VC, COL_ZC, COL_QM, COL_ZM, COL_GATE) = range(14)

ROWS = BATCH * SEQ
F32 = jnp.float32
BF16 = jnp.bfloat16

VMEM_LIMIT_BYTES = 56 * 1024 * 1024

NT_DIMS = (((1,), (1,)), ((), ()))


def _cparams(semantics):
    return pltpu.CompilerParams(dimension_semantics=semantics, vmem_limit_bytes=VMEM_LIMIT_BYTES)


def _rmsnorm_kernel(x_ref, g_ref, o_ref):
    x = x_ref[...].astype(F32)
    ms = jnp.mean(x * x, axis=-1, keepdims=True)
    o_ref[...] = (x * lax.rsqrt(ms + RMS_EPS) * g_ref[...]).astype(o_ref.dtype)


def rmsnorm(x, g, out_dtype, tile_rows=256):
    rows, d = x.shape
    return pl.pallas_call(
        _rmsnorm_kernel,
        out_shape=jax.ShapeDtypeStruct((rows, d), out_dtype),
        grid=(rows // tile_rows,),
        in_specs=[pl.BlockSpec((tile_rows, d), lambda i: (i, 0)),
                  pl.BlockSpec((1, d), lambda i: (0, 0))],
        out_specs=pl.BlockSpec((tile_rows, d), lambda i: (i, 0)),
        compiler_params=_cparams(("parallel",)),
        name="rmsnorm",
    )(x, g.reshape(1, d).astype(F32))


def _mm_kernel(a_ref, b_ref, o_ref):
    o_ref[...] = jnp.dot(a_ref[...], b_ref[...], preferred_element_type=F32).astype(o_ref.dtype)


def _mm_residual_kernel(a_ref, b_ref, r_ref, o_ref):
    acc = jnp.dot(a_ref[...], b_ref[...], preferred_element_type=F32)
    o_ref[...] = (r_ref[...] + acc).astype(o_ref.dtype)


def matmul(a, b, out_dtype, tm, tn, residual=None, name="matmul"):
    m, k = a.shape
    _, n = b.shape
    in_specs = [pl.BlockSpec((tm, k), lambda i, j: (i, 0)),
                pl.BlockSpec((k, tn), lambda i, j: (0, j))]
    args = [a, b]
    body = _mm_kernel
    if residual is not None:
        in_specs.append(pl.BlockSpec((tm, tn), lambda i, j: (i, j)))
        args.append(residual)
        body = _mm_residual_kernel
    return pl.pallas_call(
        body,
        out_shape=jax.ShapeDtypeStruct((m, n), out_dtype),
        grid=(m // tm, n // tn),
        in_specs=in_specs,
        out_specs=pl.BlockSpec((tm, tn), lambda i, j: (i, j)),
        compiler_params=_cparams(("parallel", "arbitrary")),
        name=name,
    )(*args)


MOBA_TQ = MOBA_BLOCK


def _moba_kernel(slopes_ref, q_ref, k_ref, v_ref, z_ref, o_ref, kb_ref, vb_ref, km_ref):
    h = pl.program_id(1)
    i = pl.program_id(2)
    tq = MOBA_TQ

    @pl.when(i == 0)
    def _():
        kb_ref[...] = k_ref[...].astype(BF16)
        vb_ref[...] = v_ref[...].astype(BF16)
        km_ref[...] = jnp.zeros_like(km_ref)
        for n in range(N_MOBA_BLOCKS):
            km_ref[n:n + 1, :] = jnp.mean(k_ref[n * MOBA_BLOCK:(n + 1) * MOBA_BLOCK, :], axis=0, keepdims=True)

    qf = q_ref[...]
    gate = lax.dot_general(qf, km_ref[...], NT_DIMS, precision=lax.Precision.HIGHEST,
                           preferred_element_type=F32)
    lane = lax.broadcasted_iota(jnp.int32, (tq, HEAD_DIM), 1)
    past = lane < i
    g = jnp.where(past, gate, NEG)
    rank = jnp.zeros((tq, HEAD_DIM), jnp.int32)
    for m in range(N_MOBA_BLOCKS):
        col = g[:, m:m + 1]
        beats = jnp.where(col > g, 1, jnp.where(col == g, jnp.where(lane > m, 1, 0), 0))
        rank = rank + beats
    sel = jnp.where(past, jnp.where(rank < MOBA_TOPK, 1.0, 0.0), 0.0)

    q = qf.astype(BF16)
    slope = slopes_ref[B_HEADS + h]
    scale = HEAD_DIM ** -0.5
    r_idx = lax.broadcasted_iota(jnp.int32, (tq, MOBA_BLOCK), 0)
    c_idx = lax.broadcasted_iota(jnp.int32, (tq, MOBA_BLOCK), 1)
    dloc = (r_idx - c_idx).astype(F32)

    own = pl.multiple_of(i * MOBA_BLOCK, MOBA_BLOCK)
    s = lax.dot_general(q, kb_ref[pl.ds(own, MOBA_BLOCK), :], NT_DIMS, preferred_element_type=F32) * scale
    s = s - slope * dloc
    s = jnp.where(dloc >= 0, s, NEG)
    m0 = jnp.max(s, axis=-1, keepdims=True)
    p = jnp.exp(s - m0)
    l0 = jnp.sum(p, axis=-1, keepdims=True)
    acc0 = jnp.dot(p.astype(BF16), vb_ref[pl.ds(own, MOBA_BLOCK), :], preferred_element_type=F32)

    def body(n, carry):
        m_prev, l_prev, acc = carry
        start = pl.multiple_of(n * MOBA_BLOCK, MOBA_BLOCK)
        sn = lax.dot_general(q, kb_ref[pl.ds(start, MOBA_BLOCK), :], NT_DIMS, preferred_element_type=F32) * scale
        off = ((i - n) * MOBA_BLOCK).astype(F32)
        sn = sn - slope * (dloc + off)
        sel_n = jnp.sum(jnp.where(lane == n, sel, 0.0), axis=-1, keepdims=True)
        sn = jnp.where(sel_n > 0.0, sn, NEG)
        m_new = jnp.maximum(m_prev, jnp.max(sn, axis=-1, keepdims=True))
        alpha = jnp.exp(m_prev - m_new)
        pn = jnp.exp(sn - m_new)
        l_new = alpha * l_prev + jnp.sum(pn, axis=-1, keepdims=True)
        acc_new = alpha * acc + jnp.dot(pn.astype(BF16), vb_ref[pl.ds(start, MOBA_BLOCK), :],
                                        preferred_element_type=F32)
        return m_new, l_new, acc_new

    _, l_fin, acc_fin = lax.fori_loop(0, i, body, (m0, l0, acc0))
    o_ref[...] = ((acc_fin / l_fin) * jax.nn.silu(z_ref[...])).astype(o_ref.dtype)


def moba_branch(proj, slopes):
    hpb = BW // HEAD_DIM
    nq = SEQ // MOBA_TQ
    return pl.pallas_call(
        _moba_kernel,
        out_shape=jax.ShapeDtypeStruct((ROWS, BW), BF16),
        grid=(BATCH, A_HEADS, nq),
        in_specs=[
            pl.BlockSpec(memory_space=pltpu.SMEM),
            pl.BlockSpec((MOBA_TQ, HEAD_DIM), lambda b, h, i: (b * nq + i, COL_QA * hpb + h)),
            pl.BlockSpec((SEQ, HEAD_DIM), lambda b, h, i: (b, COL_KA * hpb + h)),
            pl.BlockSpec((SEQ, HEAD_DIM), lambda b, h, i: (b, COL_VA * hpb + h)),
            pl.BlockSpec((MOBA_TQ, HEAD_DIM), lambda b, h, i: (b * nq + i, COL_ZA * hpb + h)),
        ],
        out_specs=pl.BlockSpec((MOBA_TQ, HEAD_DIM), lambda b, h, i: (b * nq + i, h)),
        scratch_shapes=[pltpu.VMEM((SEQ, HEAD_DIM), BF16),
                        pltpu.VMEM((SEQ, HEAD_DIM), BF16),
                        pltpu.VMEM((HEAD_DIM, HEAD_DIM), F32)],
        compiler_params=_cparams(("parallel", "parallel", "arbitrary")),
        name="moba_branch",
    )(slopes, proj, proj, proj, proj)


DIFF_TQ = 256
DIFF_TK = 256


def _diff_kernel(slopes_ref, lq1_ref, lk1_ref, lq2_ref, lk2_ref, subg_ref, q_ref, k_ref, v_ref, z_ref,
                 o_ref, kb_ref, vb_ref, acc_ref, *, lam_init):
    h = pl.program_id(1)
    i = pl.program_id(2)
    tq, tk = DIFF_TQ, DIFF_TK

    @pl.when(i == 0)
    def _():
        kb_ref[...] = k_ref[...].astype(BF16)
        vb_ref[...] = v_ref[...].astype(BF16)

    q = q_ref[...].astype(BF16)
    slope = slopes_ref[h]
    scale = HEAD_DIM ** -0.5
    r_idx = lax.broadcasted_iota(jnp.int32, (tq, tk), 0)
    c_idx = lax.broadcasted_iota(jnp.int32, (tq, tk), 1)
    dloc = (r_idx - c_idx).astype(F32)

    def scores(mp, start, bias):
        qm = q[:, mp * HEAD_DIM:(mp + 1) * HEAD_DIM]
        km = kb_ref[pl.ds(start, tk), mp * HEAD_DIM:(mp + 1) * HEAD_DIM]
        return lax.dot_general(qm, km, NT_DIMS, preferred_element_type=F32) * scale - bias

    own = pl.multiple_of(i * tk, tk)
    bias_own = slope * dloc
    v_own = vb_ref[pl.ds(own, tk), :]
    init = []
    for mp in range(2):
        s = jnp.where(dloc >= 0, scores(mp, own, bias_own), NEG)
        m0 = jnp.max(s, axis=-1, keepdims=True)
        p = jnp.exp(s - m0)
        init += [m0, jnp.sum(p, axis=-1, keepdims=True)]
        acc_ref[mp] = jnp.dot(p.astype(BF16), v_own, preferred_element_type=F32)

    def body(n, carry):
        start = pl.multiple_of(n * tk, tk)
        bias = slope * (dloc + ((i - n) * tk).astype(F32))
        v_n = vb_ref[pl.ds(start, tk), :]
        out = []
        for mp in range(2):
            m_prev, l_prev = carry[2 * mp], carry[2 * mp + 1]
            s = scores(mp, start, bias)
            m_new = jnp.maximum(m_prev, jnp.max(s, axis=-1, keepdims=True))
            alpha = jnp.exp(m_prev - m_new)
            p = jnp.exp(s - m_new)
            out += [m_new, alpha * l_prev + jnp.sum(p, axis=-1, keepdims=True)]
            acc_ref[mp] = alpha * acc_ref[mp] + jnp.dot(p.astype(BF16), v_n, preferred_element_type=F32)
        return tuple(out)

    _, l1, _, l2 = lax.fori_loop(0, i, body, tuple(init))

    lam = (jnp.exp(jnp.sum(lq1_ref[...] * lk1_ref[...], axis=-1, keepdims=True))
           - jnp.exp(jnp.sum(lq2_ref[...] * lk2_ref[...], axis=-1, keepdims=True)) + lam_init)
    ob = acc_ref[0] / l1 - lam * (acc_ref[1] / l2)
    ms = jnp.mean(ob * ob, axis=-1, keepdims=True)
    ob = ob * lax.rsqrt(ms + RMS_EPS) * subg_ref[...]
    ob = ob * (1.0 - lam_init)
    o_ref[...] = (ob * jax.nn.silu(z_ref[...])).astype(o_ref.dtype)


def diff_branch(proj, slopes, lq1, lk1, lq2, lk2, subln_g, lam_init):
    hpb = BW // B_V_DIM
    nq = SEQ // DIFF_TQ
    vec = lambda a: a.reshape(1, -1).astype(F32)
    small = lambda n: pl.BlockSpec((1, n), lambda b, h, i: (0, 0))
    return pl.pallas_call(
        functools.partial(_diff_kernel, lam_init=lam_init),
        out_shape=jax.ShapeDtypeStruct((ROWS, BW), BF16),
        grid=(BATCH, B_HEADS, nq),
        in_specs=[
            pl.BlockSpec(memory_space=pltpu.SMEM),
            small(HEAD_DIM), small(HEAD_DIM), small(HEAD_DIM), small(HEAD_DIM), small(B_V_DIM),
            pl.BlockSpec((DIFF_TQ, B_V_DIM), lambda b, h, i: (b * nq + i, COL_QB * hpb + h)),
            pl.BlockSpec((SEQ, B_V_DIM), lambda b, h, i: (b, COL_KB * hpb + h)),
            pl.BlockSpec((SEQ, B_V_DIM), lambda b, h, i: (b, COL_VB * hpb + h)),
            pl.BlockSpec((DIFF_TQ, B_V_DIM), lambda b, h, i: (b * nq + i, COL_ZB * hpb + h)),
        ],
        out_specs=pl.BlockSpec((DIFF_TQ, B_V_DIM), lambda b, h, i: (b * nq + i, h)),
        scratch_shapes=[pltpu.VMEM((SEQ, B_V_DIM), BF16),
                        pltpu.VMEM((SEQ, B_V_DIM), BF16),
                        pltpu.VMEM((2, DIFF_TQ, B_V_DIM), F32)],
        compiler_params=_cparams(("parallel", "parallel", "arbitrary")),
        name="diff_branch",
    )(slopes, vec(lq1), vec(lk1), vec(lq2), vec(lk2), vec(subln_g), proj, proj, proj, proj)


def _sgu_kernel(u_ref, v_ref, z_ref, lng_ref, lnb_ref, w_ref, bt_ref, o_ref):
    v = jax.nn.gelu(v_ref[...])
    mu = jnp.mean(v, axis=-1, keepdims=True)
    var = jnp.mean(jnp.square(v - mu), axis=-1, keepdims=True)
    vn = ((v - mu) * lax.rsqrt(var + LN_EPS) * lng_ref[...] + lnb_ref[...]).astype(BF16)
    r_idx = lax.broadcasted_iota(jnp.int32, (SGU_CHUNK, SGU_CHUNK), 0)
    c_idx = lax.broadcasted_iota(jnp.int32, (SGU_CHUNK, SGU_CHUNK), 1)
    causal = r_idx >= c_idx
    for g in range(SGU_GROUPS):
        cols = slice(g * SGU_GROUP_CH, (g + 1) * SGU_GROUP_CH)
        w = jnp.where(causal, w_ref[g], 0.0).astype(BF16)
        mixed = jnp.dot(w, vn[:, cols], preferred_element_type=F32) + bt_ref[:, g:g + 1]
        u = jax.nn.gelu(u_ref[:, cols])
        o_ref[:, cols] = (u * mixed * jax.nn.silu(z_ref[:, cols])).astype(o_ref.dtype)


def sgu_branch(proj, ln_g, ln_b, w_s, b_s):
    n_chunks = ROWS // SGU_CHUNK
    seg = lambda c: pl.BlockSpec((SGU_CHUNK, BW), lambda t: (t, c))
    return pl.pallas_call(
        _sgu_kernel,
        out_shape=jax.ShapeDtypeStruct((ROWS, BW), BF16),
        grid=(n_chunks,),
        in_specs=[seg(COL_UC), seg(COL_VC), seg(COL_ZC),
                  pl.BlockSpec((1, BW), lambda t: (0, 0)),
                  pl.BlockSpec((1, BW), lambda t: (0, 0)),
                  pl.BlockSpec((SGU_GROUPS, SGU_CHUNK, SGU_CHUNK), lambda t: (0, 0, 0)),
                  pl.BlockSpec((SGU_CHUNK, SGU_GROUPS), lambda t: (0, 0))],
        out_specs=pl.BlockSpec((SGU_CHUNK, BW), lambda t: (t, 0)),
        compiler_params=_cparams(("parallel",)),
        name="sgu_branch",
    )(proj, proj, proj, ln_g.reshape(1, BW), ln_b.reshape(1, BW), w_s, b_s.T)


MEM_TQ = 512


def _mem_kernel(q_ref, k_ref, v_ref, z_ref, o_ref):
    scale = M_HEAD_DIM ** -0.5
    s = lax.dot_general(q_ref[...].astype(BF16), k_ref[...], NT_DIMS, preferred_element_type=F32) * scale
    m = jnp.max(s, axis=-1, keepdims=True)
    p = jnp.exp(s - m)
    l = jnp.sum(p, axis=-1, keepdims=True)
    out = jnp.dot(p.astype(BF16), v_ref[...], preferred_element_type=F32) / l
    o_ref[...] = (out * jax.nn.silu(z_ref[...])).astype(o_ref.dtype)


def mem_branch(proj, mem_kv):
    hpb = BW // M_HEAD_DIM
    nq = SEQ // MEM_TQ
    return pl.pallas_call(
        _mem_kernel,
        out_shape=jax.ShapeDtypeStruct((ROWS, BW), BF16),
        grid=(BATCH, M_HEADS, nq),
        in_specs=[
            pl.BlockSpec((MEM_TQ, M_HEAD_DIM), lambda b, h, i: (b * nq + i, COL_QM * hpb + h)),
            pl.BlockSpec((MEM_LEN, M_HEAD_DIM), lambda b, h, i: (b, h)),
            pl.BlockSpec((MEM_LEN, M_HEAD_DIM), lambda b, h, i: (b, M_HEADS + h)),
            pl.BlockSpec((MEM_TQ, M_HEAD_DIM), lambda b, h, i: (b * nq + i, COL_ZM * hpb + h)),
        ],
        out_specs=pl.BlockSpec((MEM_TQ, M_HEAD_DIM), lambda b, h, i: (b * nq + i, h)),
        compiler_params=_cparams(("parallel", "parallel", "parallel")),
        name="mem_branch",
    )(proj, mem_kv, mem_kv, proj)


MERGE_TM = 256
MERGE_TN = 1024


def _merge_kernel(ya_ref, yb_ref, yc_ref, ym_ref, wb_ref, g0_ref, g1_ref, g2_ref, g3_ref, o_ref):
    ys = (ya_ref, yb_ref, yc_ref, ym_ref)
    gs = (g0_ref, g1_ref, g2_ref, g3_ref)
    merged = None
    for br in range(N_BRANCH):
        term = jax.nn.sigmoid(gs[br][...]) * jnp.dot(ys[br][...], wb_ref[br], preferred_element_type=F32)
        merged = term if merged is None else merged + term
    o_ref[...] = merged.astype(o_ref.dtype)


def gated_merge(ya, yb, yc, ym, w_branch, proj):
    tm, tn = MERGE_TM, MERGE_TN
    gate_blocks = D_MODEL // tn
    y_spec = pl.BlockSpec((tm, BW), lambda j, i: (i, 0))
    gate_spec = lambda br: pl.BlockSpec(
        (tm, tn), lambda j, i: (i, (COL_GATE * BW + br * D_MODEL) // tn + j))
    return pl.pallas_call(
        _merge_kernel,
        out_shape=jax.ShapeDtypeStruct((ROWS, D_MODEL), BF16),
        grid=(gate_blocks, ROWS // tm),
        in_specs=[y_spec, y_spec, y_spec, y_spec,
                  pl.BlockSpec((N_BRANCH, BW, tn), lambda j, i: (0, 0, j)),
                  gate_spec(0), gate_spec(1), gate_spec(2), gate_spec(3)],
        out_specs=pl.BlockSpec((tm, tn), lambda j, i: (i, j)),
        compiler_params=_cparams(("parallel", "arbitrary")),
        name="gated_merge",
    )(ya, yb, yc, ym, w_branch, proj, proj, proj, proj)


def kernel(x, mem, norm_g, w_in, mem_norm_g, w_mem_kv, diff_lam_q1, diff_lam_k1, diff_lam_q2, diff_lam_k2,
           diff_subln_g, sgu_ln_g, sgu_ln_b, sgu_w, sgu_b, w_branch, w_out, final_g):
    xf = x.reshape(ROWS, D_MODEL)
    memf = mem.reshape(BATCH * MEM_LEN, D_MODEL)
    slopes = jnp.exp2(-8.0 * jnp.arange(1, N_ALIBI_HEADS + 1, dtype=F32) / N_ALIBI_HEADS)
    for l in range(DEPTH):
        lam_init = 0.8 - 0.6 * math.exp(-0.3 * l)
        h = rmsnorm(xf, norm_g[l], BF16)
        proj = matmul(h, w_in[l].astype(BF16), F32, tm=1024, tn=1024, name="in_proj")
        hm = rmsnorm(memf, mem_norm_g[l], BF16)
        mem_kv = matmul(hm, w_mem_kv[l].astype(BF16), BF16, tm=1024, tn=1024, name="mem_kv_proj")
        ya = moba_branch(proj, slopes)
        yb = diff_branch(proj, slopes, diff_lam_q1[l], diff_lam_k1[l], diff_lam_q2[l], diff_lam_k2[l],
                         diff_subln_g[l], lam_init)
        yc = sgu_branch(proj, sgu_ln_g[l], sgu_ln_b[l], sgu_w[l], sgu_b[l])
        ym = mem_branch(proj, mem_kv)
        merged = gated_merge(ya, yb, yc, ym, w_branch[l].astype(BF16), proj)
        xf = matmul(merged, w_out[l].astype(BF16), F32, tm=512, tn=1024, residual=xf, name="out_proj")
    out = rmsnorm(xf, final_g, F32)
    return out.reshape(BATCH, SEQ, D_MODEL)
```

```python
import functools
import math

import jax
import jax.numpy as jnp
from jax import lax
from jax.experimental import pallas as pl
from jax.experimental.pallas import tpu as pltpu

D_MODEL = 4096
BATCH = 4
SEQ = 2048
DEPTH = 2
MEM_LEN = 256
HEAD_DIM = 128
BW = D_MODEL // 4
N_BRANCH = 4
A_HEADS = BW // HEAD_DIM
MOBA_BLOCK = 256
MOBA_TOPK = 3
B_HEADS = BW // (2 * HEAD_DIM)
B_V_DIM = 2 * HEAD_DIM
SGU_CHUNK = 128
SGU_GROUP_CH = 128
SGU_GROUPS = BW // SGU_GROUP_CH
M_HEADS = 4
M_HEAD_DIM = BW // M_HEADS
N_ALIBI_HEADS = A_HEADS + B_HEADS
RMS_EPS = 1e-6
LN_EPS = 1e-5
NEG = -1e30
W_IN_COLS = 13 * BW + N_BRANCH * D_MODEL
N_MOBA_BLOCKS = SEQ // MOBA_BLOCK

(COL_QA, COL_KA, COL_VA, COL_ZA, COL_QB, COL_KB, COL_VB, COL_ZB,
 COL_UC, COL_VC, COL_ZC, COL_QM, COL_ZM, COL_GATE) = range(14)

ROWS = BATCH * SEQ
F32 = jnp.float32
BF16 = jnp.bfloat16
LOG2E = math.log2(math.e)

VMEM_LIMIT_BYTES = 58 * 1024 * 1024
SUBLANES = 8

NT_DIMS = (((1,), (1,)), ((), ()))


def _cparams(semantics):
    return pltpu.CompilerParams(dimension_semantics=semantics, vmem_limit_bytes=VMEM_LIMIT_BYTES)


def _rmsnorm_kernel(x_ref, g_ref, o_ref):
    x = x_ref[...].astype(F32)
    ms = jnp.mean(x * x, axis=-1, keepdims=True)
    o_ref[...] = (x * lax.rsqrt(ms + RMS_EPS) * g_ref[...]).astype(o_ref.dtype)


def rmsnorm(x, g, out_dtype, tile_rows=256):
    rows, d = x.shape
    return pl.pallas_call(
        _rmsnorm_kernel,
        out_shape=jax.ShapeDtypeStruct((rows, d), out_dtype),
        grid=(rows // tile_rows,),
        in_specs=[pl.BlockSpec((tile_rows, d), lambda i: (i, 0)),
                  pl.BlockSpec((1, d), lambda i: (0, 0))],
        out_specs=pl.BlockSpec((tile_rows, d), lambda i: (i, 0)),
        compiler_params=_cparams(("parallel",)),
        name="rmsnorm",
    )(x, g.reshape(1, d).astype(F32))


def _mm_kernel(a_ref, b_ref, o_ref):
    o_ref[...] = jnp.dot(a_ref[...], b_ref[...], preferred_element_type=F32).astype(o_ref.dtype)


def matmul(a, b, out_dtype, tm, tn, name="matmul"):
    m, k = a.shape
    _, n = b.shape
    return pl.pallas_call(
        _mm_kernel,
        out_shape=jax.ShapeDtypeStruct((m, n), out_dtype),
        grid=(m // tm, n // tn),
        in_specs=[pl.BlockSpec((tm, k), lambda i, j: (i, 0)),
                  pl.BlockSpec((k, tn), lambda i, j: (0, j))],
        out_specs=pl.BlockSpec((tm, tn), lambda i, j: (i, j)),
        compiler_params=_cparams(("parallel", "arbitrary")),
        name=name,
    )(a, b)


def _mm_wcast_kernel(a_ref, w_ref, o_ref, wb_ref):
    @pl.when(pl.program_id(1) == 0)
    def _():
        wb_ref[...] = w_ref[...].astype(BF16)

    o_ref[...] = jnp.dot(a_ref[...], wb_ref[...], preferred_element_type=F32).astype(o_ref.dtype)


def _mm_wcast_residual_kernel(a_ref, w_ref, r_ref, o_ref, wb_ref):
    @pl.when(pl.program_id(1) == 0)
    def _():
        wb_ref[...] = w_ref[...].astype(BF16)

    acc = jnp.dot(a_ref[...], wb_ref[...], preferred_element_type=F32)
    o_ref[...] = (r_ref[...] + acc).astype(o_ref.dtype)


def matmul_f32_weights(a, w_stack, layer, out_dtype, tm, tn, residual=None, name="matmul_w"):
    m, k = a.shape
    n = w_stack.shape[-1]
    in_specs = [pl.BlockSpec((tm, k), lambda j, i: (i, 0)),
                pl.BlockSpec((None, k, tn), lambda j, i: (layer, 0, j))]
    args = [a, w_stack]
    body = _mm_wcast_kernel
    if residual is not None:
        in_specs.append(pl.BlockSpec((tm, tn), lambda j, i: (i, j)))
        args.append(residual)
        body = _mm_wcast_residual_kernel
    return pl.pallas_call(
        body,
        out_shape=jax.ShapeDtypeStruct((m, n), out_dtype),
        grid=(n // tn, m // tm),
        in_specs=in_specs,
        out_specs=pl.BlockSpec((tm, tn), lambda j, i: (i, j)),
        scratch_shapes=[pltpu.VMEM((k, tn), BF16)],
        compiler_params=_cparams(("parallel", "arbitrary")),
        name=name,
    )(*args)


MOBA_TQ = MOBA_BLOCK


def _moba_kernel(slopes_ref, q_ref, k_ref, v_ref, z_ref, o_ref, qa_ref, ka_ref, va_ref):
    h = pl.program_id(1)
    nb, blk, hd = N_MOBA_BLOCKS, MOBA_BLOCK, HEAD_DIM

    k = k_ref[...]
    ka_ref[:, :hd] = k
    blk_shift = blk.bit_length() - 1
    key_blk = lax.broadcasted_iota(jnp.int32, (SEQ, hd), 0) >> blk_shift
    ka_ref[:, hd:] = jnp.where(key_blk == lax.broadcasted_iota(jnp.int32, (SEQ, hd), 1), 1.0, 0.0).astype(BF16)
    va_ref[:, :hd] = v_ref[...]
    va_ref[:, hd:] = jnp.ones((SEQ, hd), BF16)

    km = jnp.concatenate(
        [jnp.mean(k[n * blk:(n + 1) * blk, :].astype(F32), axis=0, keepdims=True) for n in range(nb)], axis=0)
    q = q_ref[...]
    gate_t = lax.dot_general(km, q.astype(F32), NT_DIMS, precision=lax.Precision.HIGHEST,
                             preferred_element_type=F32)
    n_idx = lax.broadcasted_iota(jnp.int32, (nb, SEQ), 0)
    q_blk = lax.broadcasted_iota(jnp.int32, (nb, SEQ), 1) >> blk_shift
    past = n_idx < q_blk
    g = jnp.where(past, gate_t, NEG)
    rank = jnp.zeros((nb, SEQ), jnp.int32)
    for m in range(nb):
        row = g[m:m + 1, :]
        rank = rank + jnp.where(row > g, 1, jnp.where(row == g, jnp.where(n_idx > m, 1, 0), 0))
    mask_t = jnp.where(past, jnp.where(rank < MOBA_TOPK, 0.0, NEG),
                       jnp.where(n_idx == q_blk, 0.0, NEG))
    mask_rows = jnp.concatenate([mask_t, jnp.zeros((hd - nb, SEQ), F32)], axis=0).T
    qa_ref[:, :hd] = q
    qa_ref[:, hd:] = mask_rows.astype(BF16)

    slope2 = slopes_ref[B_HEADS + h] * LOG2E
    c1 = (hd ** -0.5) * LOG2E
    r_idx = lax.broadcasted_iota(jnp.int32, (MOBA_TQ, blk), 0)
    c_idx = lax.broadcasted_iota(jnp.int32, (MOBA_TQ, blk), 1)
    causal = r_idx >= c_idx
    for i in range(nb):
        rows = slice(i * MOBA_TQ, (i + 1) * MOBA_TQ)
        width = (i + 1) * blk
        s = lax.dot_general(qa_ref[rows, :], ka_ref[:width, :], NT_DIMS, preferred_element_type=F32)
        key_pos = lax.broadcasted_iota(jnp.int32, (1, width), 1) - i * blk
        t = s * c1 + slope2 * key_pos.astype(F32)
        t_own = jnp.where(causal, t[:, i * blk:], NEG)
        t = t_own if i == 0 else jnp.concatenate([t[:, :i * blk], t_own], axis=1)
        p = jnp.exp2(t - jnp.max(t, axis=-1, keepdims=True))
        oa = jnp.dot(p.astype(BF16), va_ref[:width, :], preferred_element_type=F32)
        out = oa[:, :hd] / oa[:, hd:]
        o_ref[rows, :] = (out * jax.nn.silu(z_ref[rows, :].astype(F32))).astype(o_ref.dtype)


def moba_branch(proj, slopes):
    hpb = BW // HEAD_DIM
    seg = lambda c: pl.BlockSpec((SEQ, HEAD_DIM), lambda b, h: (b, c * hpb + h))
    return pl.pallas_call(
        _moba_kernel,
        out_shape=jax.ShapeDtypeStruct((ROWS, BW), BF16),
        grid=(BATCH, A_HEADS),
        in_specs=[pl.BlockSpec(memory_space=pltpu.SMEM), seg(COL_QA), seg(COL_KA), seg(COL_VA), seg(COL_ZA)],
        out_specs=pl.BlockSpec((SEQ, HEAD_DIM), lambda b, h: (b, h)),
        scratch_shapes=[pltpu.VMEM((SEQ, 2 * HEAD_DIM), BF16),
                        pltpu.VMEM((SEQ, 2 * HEAD_DIM), BF16),
                        pltpu.VMEM((SEQ, 2 * HEAD_DIM), BF16)],
        compiler_params=_cparams(("parallel", "parallel")),
        name="moba_branch",
    )(slopes, proj, proj, proj, proj)


DIFF_TQ = 256


def _diff_kernel(slopes_ref, lq1_ref, lk1_ref, lq2_ref, lk2_ref, subg_ref, q_ref, k_ref, v_ref, z_ref,
                 o_ref, *, lam_init):
    h = pl.program_id(1)
    tq, hd = DIFF_TQ, HEAD_DIM
    slope2 = slopes_ref[h] * LOG2E
    c1 = (hd ** -0.5) * LOG2E
    lam = (jnp.exp(jnp.sum(lq1_ref[...] * lk1_ref[...], axis=-1, keepdims=True))
           - jnp.exp(jnp.sum(lq2_ref[...] * lk2_ref[...], axis=-1, keepdims=True)) + lam_init)
    r_idx = lax.broadcasted_iota(jnp.int32, (tq, tq), 0)
    c_idx = lax.broadcasted_iota(jnp.int32, (tq, tq), 1)
    causal = r_idx >= c_idx
    for i in range(SEQ // tq):
        rows = slice(i * tq, (i + 1) * tq)
        width = (i + 1) * tq
        key_bias = slope2 * (lax.broadcasted_iota(jnp.int32, (1, width), 1) - i * tq).astype(F32)
        v = v_ref[:width, :]
        outs = []
        for mp in range(2):
            cols = slice(mp * hd, (mp + 1) * hd)
            s = lax.dot_general(q_ref[rows, cols], k_ref[:width, cols], NT_DIMS, preferred_element_type=F32)
            t = s * c1 + key_bias
            t_own = jnp.where(causal, t[:, i * tq:], NEG)
            t = t_own if i == 0 else jnp.concatenate([t[:, :i * tq], t_own], axis=1)
            p = jnp.exp2(t - jnp.max(t, axis=-1, keepdims=True))
            l = jnp.sum(p, axis=-1, keepdims=True)
            outs.append(jnp.dot(p.astype(BF16), v, preferred_element_type=F32) / l)
        ob = outs[0] - lam * outs[1]
        ms = jnp.mean(ob * ob, axis=-1, keepdims=True)
        ob = ob * lax.rsqrt(ms + RMS_EPS) * subg_ref[...]
        ob = ob * (1.0 - lam_init)
        o_ref[rows, :] = (ob * jax.nn.silu(z_ref[rows, :].astype(F32))).astype(o_ref.dtype)


def diff_branch(proj, slopes, lq1, lk1, lq2, lk2, subln_g, lam_init):
    hpb = BW // B_V_DIM
    vec = lambda a: a.reshape(1, -1).astype(F32)
    small = lambda n: pl.BlockSpec((1, n), lambda b, h: (0, 0))
    seg = lambda c: pl.BlockSpec((SEQ, B_V_DIM), lambda b, h: (b, c * hpb + h))
    return pl.pallas_call(
        functools.partial(_diff_kernel, lam_init=lam_init),
        out_shape=jax.ShapeDtypeStruct((ROWS, BW), BF16),
        grid=(BATCH, B_HEADS),
        in_specs=[pl.BlockSpec(memory_space=pltpu.SMEM),
                  small(HEAD_DIM), small(HEAD_DIM), small(HEAD_DIM), small(HEAD_DIM), small(B_V_DIM),
                  seg(COL_QB), seg(COL_KB), seg(COL_VB), seg(COL_ZB)],
        out_specs=pl.BlockSpec((SEQ, B_V_DIM), lambda b, h: (b, h)),
        compiler_params=_cparams(("parallel", "parallel")),
        name="diff_branch",
    )(slopes, vec(lq1), vec(lk1), vec(lq2), vec(lk2), vec(subln_g), proj, proj, proj, proj)


def _sgu_kernel(u_ref, v_ref, z_ref, lng_ref, lnb_ref, w_ref, bt_ref, o_ref):
    v = jax.nn.gelu(v_ref[...].astype(F32))
    mu = jnp.mean(v, axis=-1, keepdims=True)
    var = jnp.mean(jnp.square(v - mu), axis=-1, keepdims=True)
    vn = ((v - mu) * lax.rsqrt(var + LN_EPS) * lng_ref[...] + lnb_ref[...]).astype(BF16)
    r_idx = lax.broadcasted_iota(jnp.int32, (SGU_CHUNK, SGU_CHUNK), 0)
    c_idx = lax.broadcasted_iota(jnp.int32, (SGU_CHUNK, SGU_CHUNK), 1)
    causal = r_idx >= c_idx
    for g in range(SGU_GROUPS):
        cols = slice(g * SGU_GROUP_CH, (g + 1) * SGU_GROUP_CH)
        w = jnp.where(causal, w_ref[g], 0.0).astype(BF16)
        mixed = jnp.dot(w, vn[:, cols], preferred_element_type=F32) + bt_ref[:, g:g + 1]
        u = jax.nn.gelu(u_ref[:, cols].astype(F32))
        o_ref[:, cols] = (u * mixed * jax.nn.silu(z_ref[:, cols].astype(F32))).astype(o_ref.dtype)


def sgu_branch(proj, ln_g, ln_b, w_s, b_s):
    n_chunks = ROWS // SGU_CHUNK
    seg = lambda c: pl.BlockSpec((SGU_CHUNK, BW), lambda t: (t, c))
    return pl.pallas_call(
        _sgu_kernel,
        out_shape=jax.ShapeDtypeStruct((ROWS, BW), BF16),
        grid=(n_chunks,),
        in_specs=[seg(COL_UC), seg(COL_VC), seg(COL_ZC),
                  pl.BlockSpec((1, BW), lambda t: (0, 0)),
                  pl.BlockSpec((1, BW), lambda t: (0, 0)),
                  pl.BlockSpec((SGU_GROUPS, SGU_CHUNK, SGU_CHUNK), lambda t: (0, 0, 0)),
                  pl.BlockSpec((SGU_CHUNK, SGU_GROUPS), lambda t: (0, 0))],
        out_specs=pl.BlockSpec((SGU_CHUNK, BW), lambda t: (t, 0)),
        compiler_params=_cparams(("parallel",)),
        name="sgu_branch",
    )(proj, proj, proj, ln_g.reshape(1, BW), ln_b.reshape(1, BW), w_s, b_s.T)


MEM_TQ = 512


def _mem_kernel(q_ref, k_ref, v_ref, z_ref, o_ref):
    scale = M_HEAD_DIM ** -0.5
    s = lax.dot_general(q_ref[...], k_ref[...], NT_DIMS, preferred_element_type=F32) * scale
    m = jnp.max(s, axis=-1, keepdims=True)
    p = jnp.exp(s - m)
    l = jnp.sum(p, axis=-1, keepdims=True)
    out = jnp.dot(p.astype(BF16), v_ref[...], preferred_element_type=F32) / l
    o_ref[...] = (out * jax.nn.silu(z_ref[...].astype(F32))).astype(o_ref.dtype)


def mem_branch(proj, mem_kv):
    hpb = BW // M_HEAD_DIM
    nq = SEQ // MEM_TQ
    return pl.pallas_call(
        _mem_kernel,
        out_shape=jax.ShapeDtypeStruct((ROWS, BW), BF16),
        grid=(BATCH, M_HEADS, nq),
        in_specs=[
            pl.BlockSpec((MEM_TQ, M_HEAD_DIM), lambda b, h, i: (b * nq + i, COL_QM * hpb + h)),
            pl.BlockSpec((MEM_LEN, M_HEAD_DIM), lambda b, h, i: (b, h)),
            pl.BlockSpec((MEM_LEN, M_HEAD_DIM), lambda b, h, i: (b, M_HEADS + h)),
            pl.BlockSpec((MEM_TQ, M_HEAD_DIM), lambda b, h, i: (b * nq + i, COL_ZM * hpb + h)),
        ],
        out_specs=pl.BlockSpec((MEM_TQ, M_HEAD_DIM), lambda b, h, i: (b * nq + i, h)),
        compiler_params=_cparams(("parallel", "parallel", "parallel")),
        name="mem_branch",
    )(proj, mem_kv, mem_kv, proj)


MERGE_TM = 512
MERGE_TN = 1024


def _merge_kernel(ya_ref, yb_ref, yc_ref, ym_ref, wb_ref, g0_ref, g1_ref, g2_ref, g3_ref, o_ref):
    ys = (ya_ref, yb_ref, yc_ref, ym_ref)
    gs = (g0_ref, g1_ref, g2_ref, g3_ref)
    merged = None
    for br in range(N_BRANCH):
        gate = jax.nn.sigmoid(gs[br][...].astype(F32))
        term = gate * jnp.dot(ys[br][...], wb_ref[br], preferred_element_type=F32)
        merged = term if merged is None else merged + term
    o_ref[...] = merged.astype(o_ref.dtype)


def gated_merge(ya, yb, yc, ym, w_branch, proj):
    tm, tn = MERGE_TM, MERGE_TN
    gate_blocks = D_MODEL // tn
    y_spec = pl.BlockSpec((tm, BW), lambda j, i: (i, 0))
    gate_spec = lambda br: pl.BlockSpec(
        (tm, tn), lambda j, i: (i, (COL_GATE * BW + br * D_MODEL) // tn + j))
    return pl.pallas_call(
        _merge_kernel,
        out_shape=jax.ShapeDtypeStruct((ROWS, D_MODEL), BF16),
        grid=(gate_blocks, ROWS // tm),
        in_specs=[y_spec, y_spec, y_spec, y_spec,
                  pl.BlockSpec((N_BRANCH, BW, tn), lambda j, i: (0, 0, j)),
                  gate_spec(0), gate_spec(1), gate_spec(2), gate_spec(3)],
        out_specs=pl.BlockSpec((tm, tn), lambda j, i: (i, j)),
        compiler_params=_cparams(("parallel", "arbitrary")),
        name="gated_merge",
    )(ya, yb, yc, ym, w_branch, proj, proj, proj, proj)


def kernel(x, mem, norm_g, w_in, mem_norm_g, w_mem_kv, diff_lam_q1, diff_lam_k1, diff_lam_q2, diff_lam_k2,
           diff_subln_g, sgu_ln_g, sgu_ln_b, sgu_w, sgu_b, w_branch, w_out, final_g):
    xf = x.reshape(ROWS, D_MODEL)
    memf = mem.reshape(BATCH * MEM_LEN, D_MODEL)
    slopes = jnp.exp2(-8.0 * jnp.arange(1, N_ALIBI_HEADS + 1, dtype=F32) / N_ALIBI_HEADS)
    for l in range(DEPTH):
        lam_init = 0.8 - 0.6 * math.exp(-0.3 * l)
        h = rmsnorm(xf, norm_g[l], BF16)
        proj = matmul_f32_weights(h, w_in, l, BF16, tm=512, tn=1024, name="in_proj")
        hm = rmsnorm(memf, mem_norm_g[l], BF16)
        mem_kv = matmul(hm, w_mem_kv[l].astype(BF16), BF16, tm=1024, tn=1024, name="mem_kv_proj")
        ya = moba_branch(proj, slopes)
        yb = diff_branch(proj, slopes, diff_lam_q1[l], diff_lam_k1[l], diff_lam_q2[l], diff_lam_k2[l],
                         diff_subln_g[l], lam_init)
        yc = sgu_branch(proj, sgu_ln_g[l], sgu_ln_b[l], sgu_w[l], sgu_b[l])
        ym = mem_branch(proj, mem_kv)
        merged = gated_merge(ya, yb, yc, ym, w_branch[l].astype(BF16), proj)
        xf = matmul_f32_weights(merged, w_out, l, F32, tm=1024, tn=512, residual=xf, name="out_proj")
    out = rmsnorm(xf, final_g, F32)
    return out.reshape(BATCH, SEQ, D_MODEL)
```

```python
import functools
import math

import jax
import jax.numpy as jnp
from jax import lax
from jax.experimental import pallas as pl
from jax.experimental.pallas import tpu as pltpu

D_MODEL = 4096
BATCH = 4
SEQ = 2048
DEPTH = 2
MEM_LEN = 256
HEAD_DIM = 128
BW = D_MODEL // 4
N_BRANCH = 4
A_HEADS = BW // HEAD_DIM
MOBA_BLOCK = 256
MOBA_TOPK = 3
B_HEADS = BW // (2 * HEAD_DIM)
B_V_DIM = 2 * HEAD_DIM
SGU_CHUNK = 128
SGU_GROUP_CH = 128
SGU_GROUPS = BW // SGU_GROUP_CH
M_HEADS = 4
M_HEAD_DIM = BW // M_HEADS
N_ALIBI_HEADS = A_HEADS + B_HEADS
RMS_EPS = 1e-6
LN_EPS = 1e-5
NEG = -1e30
W_IN_COLS = 13 * BW + N_BRANCH * D_MODEL
N_MOBA_BLOCKS = SEQ // MOBA_BLOCK

(COL_QA, COL_KA, COL_VA, COL_ZA, COL_QB, COL_KB, COL_VB, COL_ZB,
 COL_UC, COL_VC, COL_ZC, COL_QM, COL_ZM, COL_GATE) = range(14)

ROWS = BATCH * SEQ
F32 = jnp.float32
BF16 = jnp.bfloat16
LOG2E = math.log2(math.e)

VMEM_LIMIT_BYTES = 58 * 1024 * 1024
SUBLANES = 8

NT_DIMS = (((1,), (1,)), ((), ()))


def _cparams(semantics):
    return pltpu.CompilerParams(dimension_semantics=semantics, vmem_limit_bytes=VMEM_LIMIT_BYTES)


def _rmsnorm_kernel(x_ref, g_ref, o_ref):
    x = x_ref[...].astype(F32)
    ms = jnp.mean(x * x, axis=-1, keepdims=True)
    o_ref[...] = (x * lax.rsqrt(ms + RMS_EPS) * g_ref[...]).astype(o_ref.dtype)


def rmsnorm(x, g, out_dtype, tile_rows=256):
    rows, d = x.shape
    return pl.pallas_call(
        _rmsnorm_kernel,
        out_shape=jax.ShapeDtypeStruct((rows, d), out_dtype),
        grid=(rows // tile_rows,),
        in_specs=[pl.BlockSpec((tile_rows, d), lambda i: (i, 0)),
                  pl.BlockSpec((1, d), lambda i: (0, 0))],
        out_specs=pl.BlockSpec((tile_rows, d), lambda i: (i, 0)),
        compiler_params=_cparams(("parallel",)),
        name="rmsnorm",
    )(x, g.reshape(1, d).astype(F32))


def _mm_kernel(a_ref, b_ref, o_ref):
    o_ref[...] = jnp.dot(a_ref[...], b_ref[...], preferred_element_type=F32).astype(o_ref.dtype)


def matmul(a, b, out_dtype, tm, tn, name="matmul"):
    m, k = a.shape
    _, n = b.shape
    return pl.pallas_call(
        _mm_kernel,
        out_shape=jax.ShapeDtypeStruct((m, n), out_dtype),
        grid=(m // tm, n // tn),
        in_specs=[pl.BlockSpec((tm, k), lambda i, j: (i, 0)),
                  pl.BlockSpec((k, tn), lambda i, j: (0, j))],
        out_specs=pl.BlockSpec((tm, tn), lambda i, j: (i, j)),
        compiler_params=_cparams(("parallel", "arbitrary")),
        name=name,
    )(a, b)


class _WeightStream:
    def __init__(self, w_hbm, wb_ref, stage_ref, sem, layer, tn):
        self.w_hbm, self.wb_ref, self.stage_ref, self.sem = w_hbm, wb_ref, stage_ref, sem
        self.layer, self.tn = layer, tn
        self.j, self.i = pl.program_id(0), pl.program_id(1)
        self.nj, self.ni = pl.num_programs(0), pl.num_programs(1)
        self.chunk_rows = stage_ref.shape[1]

    def _rows(self, chunk):
        return pl.ds(pl.multiple_of(chunk * self.chunk_rows, self.chunk_rows), self.chunk_rows)

    def _copy(self, block, chunk):
        cols = pl.ds(pl.multiple_of(block * self.tn, self.tn), self.tn)
        src = self.w_hbm.at[self.layer, self._rows(chunk), cols]
        slot = chunk % 2
        return pltpu.make_async_copy(src, self.stage_ref.at[slot], self.sem.at[slot])

    def _round(self, wb_slot, chunk):
        self.wb_ref[wb_slot, self._rows(chunk), :] = self.stage_ref[chunk % 2].astype(BF16)

    def _prefetch_block(self, j):
        return jnp.minimum(j + 1, self.nj - 1)

    def before_matmul(self, n_chunks):
        j, i = self.j, self.i
        step = j * self.ni + i

        @pl.when(step == 0)
        def _():
            self._copy(0, 0).start()
            for c in range(n_chunks):
                if c + 1 < n_chunks:
                    self._copy(0, c + 1).start()
                self._copy(0, c).wait()
                self._round(0, c)
            self._copy(self._prefetch_block(0), 0).start()

        self._copy(self._prefetch_block(j), i).wait()

        @pl.when(step + 1 < self.nj * self.ni)
        def _():
            wrap = i + 1 == self.ni
            i_next = jnp.where(wrap, 0, i + 1)
            j_next = jnp.where(wrap, j + 1, j)
            self._copy(self._prefetch_block(j_next), i_next).start()

        self._round((j + 1) % 2, i)
        return self.wb_ref.at[j % 2]


def _mm_wstream_kernel(a_ref, w_hbm, o_ref, wb_ref, stage_ref, sem, *, layer, n_chunks):
    stream = _WeightStream(w_hbm, wb_ref, stage_ref, sem, layer, o_ref.shape[1])
    w = stream.before_matmul(n_chunks)
    o_ref[...] = jnp.dot(a_ref[...], w[...], preferred_element_type=F32).astype(o_ref.dtype)


def _mm_wstream_residual_kernel(a_ref, w_hbm, r_ref, o_ref, wb_ref, stage_ref, sem, *, layer, n_chunks):
    stream = _WeightStream(w_hbm, wb_ref, stage_ref, sem, layer, o_ref.shape[1])
    w = stream.before_matmul(n_chunks)
    acc = jnp.dot(a_ref[...], w[...], preferred_element_type=F32)
    o_ref[...] = (r_ref[...] + acc).astype(o_ref.dtype)


def _weight_stream_scratch(k, tn, n_chunks):
    return [pltpu.VMEM((2, k, tn), BF16),
            pltpu.VMEM((2, k // n_chunks, tn), F32),
            pltpu.SemaphoreType.DMA((2,))]


def matmul_f32_weights(a, w_stack, layer, out_dtype, tm, tn, residual=None, name="matmul_w"):
    m, k = a.shape
    n = w_stack.shape[-1]
    n_chunks = m // tm
    in_specs = [pl.BlockSpec((tm, k), lambda j, i: (i, 0)),
                pl.BlockSpec(memory_space=pl.ANY)]
    args = [a, w_stack]
    body = _mm_wstream_kernel
    if residual is not None:
        in_specs.append(pl.BlockSpec((tm, tn), lambda j, i: (i, j)))
        args.append(residual)
        body = _mm_wstream_residual_kernel
    return pl.pallas_call(
        functools.partial(body, layer=layer, n_chunks=n_chunks),
        out_shape=jax.ShapeDtypeStruct((m, n), out_dtype),
        grid=(n // tn, n_chunks),
        in_specs=in_specs,
        out_specs=pl.BlockSpec((tm, tn), lambda j, i: (i, j)),
        scratch_shapes=_weight_stream_scratch(k, tn, n_chunks),
        compiler_params=_cparams(("arbitrary", "arbitrary")),
        name=name,
    )(*args)


MOBA_TQ = MOBA_BLOCK


def _moba_kernel(slopes_ref, q_ref, k_ref, v_ref, z_ref, o_ref, qa_ref, ka_ref, va_ref):
    h = pl.program_id(1)
    nb, blk, hd = N_MOBA_BLOCKS, MOBA_BLOCK, HEAD_DIM

    k = k_ref[...]
    ka_ref[:, :hd] = k
    blk_shift = blk.bit_length() - 1
    key_blk = lax.broadcasted_iota(jnp.int32, (SEQ, hd), 0) >> blk_shift
    ka_ref[:, hd:] = jnp.where(key_blk == lax.broadcasted_iota(jnp.int32, (SEQ, hd), 1), 1.0, 0.0).astype(BF16)
    va_ref[:, :hd] = v_ref[...]
    va_ref[:, hd:] = jnp.ones((SEQ, hd), BF16)

    km = jnp.concatenate(
        [jnp.mean(k[n * blk:(n + 1) * blk, :].astype(F32), axis=0, keepdims=True) for n in range(nb)], axis=0)
    q = q_ref[...]
    gate_t = lax.dot_general(km, q.astype(F32), NT_DIMS, precision=lax.Precision.HIGHEST,
                             preferred_element_type=F32)
    n_idx = lax.broadcasted_iota(jnp.int32, (nb, SEQ), 0)
    q_blk = lax.broadcasted_iota(jnp.int32, (nb, SEQ), 1) >> blk_shift
    past = n_idx < q_blk
    g = jnp.where(past, gate_t, NEG)
    rank = jnp.zeros((nb, SEQ), jnp.int32)
    for m in range(nb):
        row = g[m:m + 1, :]
        rank = rank + jnp.where(row > g, 1, jnp.where(row == g, jnp.where(n_idx > m, 1, 0), 0))
    mask_t = jnp.where(past, jnp.where(rank < MOBA_TOPK, 0.0, NEG),
                       jnp.where(n_idx == q_blk, 0.0, NEG))
    mask_rows = jnp.concatenate([mask_t, jnp.zeros((hd - nb, SEQ), F32)], axis=0).T
    qa_ref[:, :hd] = q
    qa_ref[:, hd:] = mask_rows.astype(BF16)

    slope2 = slopes_ref[B_HEADS + h] * LOG2E
    c1 = (hd ** -0.5) * LOG2E
    r_idx = lax.broadcasted_iota(jnp.int32, (MOBA_TQ, blk), 0)
    c_idx = lax.broadcasted_iota(jnp.int32, (MOBA_TQ, blk), 1)
    causal = r_idx >= c_idx
    for i in range(nb):
        rows = slice(i * MOBA_TQ, (i + 1) * MOBA_TQ)
        width = (i + 1) * blk
        s = lax.dot_general(qa_ref[rows, :], ka_ref[:width, :], NT_DIMS, preferred_element_type=F32)
        key_pos = lax.broadcasted_iota(jnp.int32, (1, width), 1) - i * blk
        t = s * c1 + slope2 * key_pos.astype(F32)
        t_own = jnp.where(causal, t[:, i * blk:], NEG)
        t = t_own if i == 0 else jnp.concatenate([t[:, :i * blk], t_own], axis=1)
        p = jnp.exp2(t - jnp.max(t, axis=-1, keepdims=True))
        oa = jnp.dot(p.astype(BF16), va_ref[:width, :], preferred_element_type=F32)
        out = oa[:, :hd] / oa[:, hd:]
        o_ref[rows, :] = (out * jax.nn.silu(z_ref[rows, :].astype(F32))).astype(o_ref.dtype)


def moba_branch(proj, slopes):
    hpb = BW // HEAD_DIM
    seg = lambda c: pl.BlockSpec((SEQ, HEAD_DIM), lambda b, h: (b, c * hpb + h))
    return pl.pallas_call(
        _moba_kernel,
        out_shape=jax.ShapeDtypeStruct((ROWS, BW), BF16),
        grid=(BATCH, A_HEADS),
        in_specs=[pl.BlockSpec(memory_space=pltpu.SMEM), seg(COL_QA), seg(COL_KA), seg(COL_VA), seg(COL_ZA)],
        out_specs=pl.BlockSpec((SEQ, HEAD_DIM), lambda b, h: (b, h)),
        scratch_shapes=[pltpu.VMEM((SEQ, 2 * HEAD_DIM), BF16),
                        pltpu.VMEM((SEQ, 2 * HEAD_DIM), BF16),
                        pltpu.VMEM((SEQ, 2 * HEAD_DIM), BF16)],
        compiler_params=_cparams(("parallel", "parallel")),
        name="moba_branch",
    )(slopes, proj, proj, proj, proj)


DIFF_TQ = 256


def _diff_kernel(slopes_ref, lq1_ref, lk1_ref, lq2_ref, lk2_ref, subg_ref, q_ref, k_ref, v_ref, z_ref,
                 o_ref, *, lam_init):
    h = pl.program_id(1)
    tq, hd = DIFF_TQ, HEAD_DIM
    slope2 = slopes_ref[h] * LOG2E
    c1 = (hd ** -0.5) * LOG2E
    lam = (jnp.exp(jnp.sum(lq1_ref[...] * lk1_ref[...], axis=-1, keepdims=True))
           - jnp.exp(jnp.sum(lq2_ref[...] * lk2_ref[...], axis=-1, keepdims=True)) + lam_init)
    r_idx = lax.broadcasted_iota(jnp.int32, (tq, tq), 0)
    c_idx = lax.broadcasted_iota(jnp.int32, (tq, tq), 1)
    causal = r_idx >= c_idx
    for i in range(SEQ // tq):
        rows = slice(i * tq, (i + 1) * tq)
        width = (i + 1) * tq
        key_bias = slope2 * (lax.broadcasted_iota(jnp.int32, (1, width), 1) - i * tq).astype(F32)
        v = v_ref[:width, :]
        outs = []
        for mp in range(2):
            cols = slice(mp * hd, (mp + 1) * hd)
            s = lax.dot_general(q_ref[rows, cols], k_ref[:width, cols], NT_DIMS, preferred_element_type=F32)
            t = s * c1 + key_bias
            t_own = jnp.where(causal, t[:, i * tq:], NEG)
            t = t_own if i == 0 else jnp.concatenate([t[:, :i * tq], t_own], axis=1)
            p = jnp.exp2(t - jnp.max(t, axis=-1, keepdims=True))
            l = jnp.sum(p, axis=-1, keepdims=True)
            outs.append(jnp.dot(p.astype(BF16), v, preferred_element_type=F32) / l)
        ob = outs[0] - lam * outs[1]
        ms = jnp.mean(ob * ob, axis=-1, keepdims=True)
        ob = ob * lax.rsqrt(ms + RMS_EPS) * subg_ref[...]
        ob = ob * (1.0 - lam_init)
        o_ref[rows, :] = (ob * jax.nn.silu(z_ref[rows, :].astype(F32))).astype(o_ref.dtype)


def diff_branch(proj, slopes, lq1, lk1, lq2, lk2, subln_g, lam_init):
    hpb = BW // B_V_DIM
    vec = lambda a: a.reshape(1, -1).astype(F32)
    small = lambda n: pl.BlockSpec((1, n), lambda b, h: (0, 0))
    seg = lambda c: pl.BlockSpec((SEQ, B_V_DIM), lambda b, h: (b, c * hpb + h))
    return pl.pallas_call(
        functools.partial(_diff_kernel, lam_init=lam_init),
        out_shape=jax.ShapeDtypeStruct((ROWS, BW), BF16),
        grid=(BATCH, B_HEADS),
        in_specs=[pl.BlockSpec(memory_space=pltpu.SMEM),
                  small(HEAD_DIM), small(HEAD_DIM), small(HEAD_DIM), small(HEAD_DIM), small(B_V_DIM),
                  seg(COL_QB), seg(COL_KB), seg(COL_VB), seg(COL_ZB)],
        out_specs=pl.BlockSpec((SEQ, B_V_DIM), lambda b, h: (b, h)),
        compiler_params=_cparams(("parallel", "parallel")),
        name="diff_branch",
    )(slopes, vec(lq1), vec(lk1), vec(lq2), vec(lk2), vec(subln_g), proj, proj, proj, proj)


SGU_TILE_CHUNKS = 4


def _sgu_kernel(u_ref, v_ref, z_ref, lng_ref, lnb_ref, w_ref, bt_ref, o_ref):
    v = jax.nn.gelu(v_ref[...].astype(F32))
    mu = jnp.mean(v, axis=-1, keepdims=True)
    var = jnp.mean(jnp.square(v - mu), axis=-1, keepdims=True)
    vn = ((v - mu) * lax.rsqrt(var + LN_EPS) * lng_ref[...] + lnb_ref[...]).astype(BF16)
    r_idx = lax.broadcasted_iota(jnp.int32, (SGU_CHUNK, SGU_CHUNK), 0)
    c_idx = lax.broadcasted_iota(jnp.int32, (SGU_CHUNK, SGU_CHUNK), 1)
    causal = r_idx >= c_idx
    chunk_rows = [slice(c * SGU_CHUNK, (c + 1) * SGU_CHUNK) for c in range(SGU_TILE_CHUNKS)]
    for g in range(SGU_GROUPS):
        cols = slice(g * SGU_GROUP_CH, (g + 1) * SGU_GROUP_CH)
        w = jnp.where(causal, w_ref[g], 0.0).astype(BF16)
        vn_g = jnp.concatenate([vn[rows, cols] for rows in chunk_rows], axis=1)
        mixed_g = jnp.dot(w, vn_g, preferred_element_type=F32) + bt_ref[:, g:g + 1]
        for c, rows in enumerate(chunk_rows):
            mixed = mixed_g[:, c * SGU_GROUP_CH:(c + 1) * SGU_GROUP_CH]
            u = jax.nn.gelu(u_ref[rows, cols].astype(F32))
            o_ref[rows, cols] = (u * mixed * jax.nn.silu(z_ref[rows, cols].astype(F32))).astype(o_ref.dtype)


def sgu_branch(proj, ln_g, ln_b, w_s, b_s):
    tile = SGU_TILE_CHUNKS * SGU_CHUNK
    seg = lambda c: pl.BlockSpec((tile, BW), lambda t: (t, c))
    return pl.pallas_call(
        _sgu_kernel,
        out_shape=jax.ShapeDtypeStruct((ROWS, BW), BF16),
        grid=(ROWS // tile,),
        in_specs=[seg(COL_UC), seg(COL_VC), seg(COL_ZC),
                  pl.BlockSpec((1, BW), lambda t: (0, 0)),
                  pl.BlockSpec((1, BW), lambda t: (0, 0)),
                  pl.BlockSpec((SGU_GROUPS, SGU_CHUNK, SGU_CHUNK), lambda t: (0, 0, 0)),
                  pl.BlockSpec((SGU_CHUNK, SGU_GROUPS), lambda t: (0, 0))],
        out_specs=pl.BlockSpec((tile, BW), lambda t: (t, 0)),
        compiler_params=_cparams(("parallel",)),
        name="sgu_branch",
    )(proj, proj, proj, ln_g.reshape(1, BW), ln_b.reshape(1, BW), w_s, b_s.T)


MEM_TQ = 512


def _mem_kernel(q_ref, k_ref, v_ref, z_ref, o_ref):
    scale = M_HEAD_DIM ** -0.5
    s = lax.dot_general(q_ref[...], k_ref[...], NT_DIMS, preferred_element_type=F32) * scale
    m = jnp.max(s, axis=-1, keepdims=True)
    p = jnp.exp(s - m)
    l = jnp.sum(p, axis=-1, keepdims=True)
    out = jnp.dot(p.astype(BF16), v_ref[...], preferred_element_type=F32) / l
    o_ref[...] = (out * jax.nn.silu(z_ref[...].astype(F32))).astype(o_ref.dtype)


def mem_branch(proj, mem_kv):
    hpb = BW // M_HEAD_DIM
    nq = SEQ // MEM_TQ
    return pl.pallas_call(
        _mem_kernel,
        out_shape=jax.ShapeDtypeStruct((ROWS, BW), BF16),
        grid=(BATCH, M_HEADS, nq),
        in_specs=[
            pl.BlockSpec((MEM_TQ, M_HEAD_DIM), lambda b, h, i: (b * nq + i, COL_QM * hpb + h)),
            pl.BlockSpec((MEM_LEN, M_HEAD_DIM), lambda b, h, i: (b, h)),
            pl.BlockSpec((MEM_LEN, M_HEAD_DIM), lambda b, h, i: (b, M_HEADS + h)),
            pl.BlockSpec((MEM_TQ, M_HEAD_DIM), lambda b, h, i: (b * nq + i, COL_ZM * hpb + h)),
        ],
        out_specs=pl.BlockSpec((MEM_TQ, M_HEAD_DIM), lambda b, h, i: (b * nq + i, h)),
        compiler_params=_cparams(("parallel", "parallel", "parallel")),
        name="mem_branch",
    )(proj, mem_kv, mem_kv, proj)


MERGE_TM = 512
MERGE_TN = 1024


def _merge_kernel(ya_ref, yb_ref, yc_ref, ym_ref, w_hbm, g0_ref, g1_ref, g2_ref, g3_ref, o_ref,
                  wb_ref, stage_ref, sem, *, layer, n_chunks):
    stream = _WeightStream(w_hbm, wb_ref, stage_ref, sem, layer, o_ref.shape[1])
    w = stream.before_matmul(n_chunks)
    ys = (ya_ref, yb_ref, yc_ref, ym_ref)
    gs = (g0_ref, g1_ref, g2_ref, g3_ref)
    merged = None
    for br in range(N_BRANCH):
        gate = jax.nn.sigmoid(gs[br][...].astype(F32))
        term = gate * jnp.dot(ys[br][...], w[br * BW:(br + 1) * BW, :], preferred_element_type=F32)
        merged = term if merged is None else merged + term
    o_ref[...] = merged.astype(o_ref.dtype)


def gated_merge(ya, yb, yc, ym, w_branch, layer, proj):
    tm, tn = MERGE_TM, MERGE_TN
    n_chunks = ROWS // tm
    k = N_BRANCH * BW
    y_spec = pl.BlockSpec((tm, BW), lambda j, i: (i, 0))
    gate_spec = lambda br: pl.BlockSpec(
        (tm, tn), lambda j, i: (i, (COL_GATE * BW + br * D_MODEL) // tn + j))
    return pl.pallas_call(
        functools.partial(_merge_kernel, layer=layer, n_chunks=n_chunks),
        out_shape=jax.ShapeDtypeStruct((ROWS, D_MODEL), BF16),
        grid=(D_MODEL // tn, n_chunks),
        in_specs=[y_spec, y_spec, y_spec, y_spec,
                  pl.BlockSpec(memory_space=pl.ANY),
                  gate_spec(0), gate_spec(1), gate_spec(2), gate_spec(3)],
        out_specs=pl.BlockSpec((tm, tn), lambda j, i: (i, j)),
        scratch_shapes=_weight_stream_scratch(k, tn, n_chunks),
        compiler_params=_cparams(("arbitrary", "arbitrary")),
        name="gated_merge",
    )(ya, yb, yc, ym, w_branch.reshape(DEPTH, k, D_MODEL), proj, proj, proj, proj)


def kernel(x, mem, norm_g, w_in, mem_norm_g, w_mem_kv, diff_lam_q1, diff_lam_k1, diff_lam_q2, diff_lam_k2,
           diff_subln_g, sgu_ln_g, sgu_ln_b, sgu_w, sgu_b, w_branch, w_out, final_g):
    xf = x.reshape(ROWS, D_MODEL)
    memf = mem.reshape(BATCH * MEM_LEN, D_MODEL)
    slopes = jnp.exp2(-8.0 * jnp.arange(1, N_ALIBI_HEADS + 1, dtype=F32) / N_ALIBI_HEADS)
    for l in range(DEPTH):
        lam_init = 0.8 - 0.6 * math.exp(-0.3 * l)
        h = rmsnorm(xf, norm_g[l], BF16)
        proj = matmul_f32_weights(h, w_in, l, BF16, tm=1024, tn=1024, name="in_proj")
        hm = rmsnorm(memf, mem_norm_g[l], BF16)
        mem_kv = matmul(hm, w_mem_kv[l].astype(BF16), BF16, tm=1024, tn=1024, name="mem_kv_proj")
        ya = moba_branch(proj, slopes)
        yb = diff_branch(proj, slopes, diff_lam_q1[l], diff_lam_k1[l], diff_lam_q2[l], diff_lam_k2[l],
                         diff_subln_g[l], lam_init)
        yc = sgu_branch(proj, sgu_ln_g[l], sgu_ln_b[l], sgu_w[l], sgu_b[l])
        ym = mem_branch(proj, mem_kv)
        merged = gated_merge(ya, yb, yc, ym, w_branch, l, proj)
        xf = matmul_f32_weights(merged, w_out, l, F32, tm=512, tn=1024, residual=xf, name="out_proj")
    out = rmsnorm(xf, final_g, F32)
    return out.reshape(BATCH, SEQ, D_MODEL)
```

```python
import functools
import math

import jax
import jax.numpy as jnp
from jax import lax
from jax.experimental import pallas as pl
from jax.experimental.pallas import tpu as pltpu

D_MODEL = 4096
BATCH = 4
SEQ = 2048
DEPTH = 2
MEM_LEN = 256
HEAD_DIM = 128
BW = D_MODEL // 4
N_BRANCH = 4
A_HEADS = BW // HEAD_DIM
MOBA_BLOCK = 256
MOBA_TOPK = 3
B_HEADS = BW // (2 * HEAD_DIM)
B_V_DIM = 2 * HEAD_DIM
SGU_CHUNK = 128
SGU_GROUP_CH = 128
SGU_GROUPS = BW // SGU_GROUP_CH
M_HEADS = 4
M_HEAD_DIM = BW // M_HEADS
N_ALIBI_HEADS = A_HEADS + B_HEADS
RMS_EPS = 1e-6
LN_EPS = 1e-5
NEG = -1e30
W_IN_COLS = 13 * BW + N_BRANCH * D_MODEL
N_MOBA_BLOCKS = SEQ // MOBA_BLOCK

(COL_QA, COL_KA, COL_VA, COL_ZA, COL_QB, COL_KB, COL_VB, COL_ZB,
 COL_UC, COL_VC, COL_ZC, COL_QM, COL_ZM, COL_GATE) = range(14)

ROWS = BATCH * SEQ
F32 = jnp.float32
BF16 = jnp.bfloat16
LOG2E = math.log2(math.e)

VMEM_LIMIT_BYTES = 58 * 1024 * 1024
SUBLANES = 8

NT_DIMS = (((1,), (1,)), ((), ()))


def _cparams(semantics):
    return pltpu.CompilerParams(dimension_semantics=semantics, vmem_limit_bytes=VMEM_LIMIT_BYTES)


def _rmsnorm_kernel(x_ref, g_ref, o_ref):
    x = x_ref[...].astype(F32)
    ms = jnp.mean(x * x, axis=-1, keepdims=True)
    o_ref[...] = (x * lax.rsqrt(ms + RMS_EPS) * g_ref[...]).astype(o_ref.dtype)


def rmsnorm(x, g, out_dtype, tile_rows=256):
    rows, d = x.shape
    return pl.pallas_call(
        _rmsnorm_kernel,
        out_shape=jax.ShapeDtypeStruct((rows, d), out_dtype),
        grid=(rows // tile_rows,),
        in_specs=[pl.BlockSpec((tile_rows, d), lambda i: (i, 0)),
                  pl.BlockSpec((1, d), lambda i: (0, 0))],
        out_specs=pl.BlockSpec((tile_rows, d), lambda i: (i, 0)),
        compiler_params=_cparams(("parallel",)),
        name="rmsnorm",
    )(x, g.reshape(1, d).astype(F32))


def _mm_kernel(a_ref, b_ref, o_ref):
    o_ref[...] = jnp.dot(a_ref[...], b_ref[...], preferred_element_type=F32).astype(o_ref.dtype)


def matmul(a, b, out_dtype, tm, tn, name="matmul"):
    m, k = a.shape
    _, n = b.shape
    return pl.pallas_call(
        _mm_kernel,
        out_shape=jax.ShapeDtypeStruct((m, n), out_dtype),
        grid=(m // tm, n // tn),
        in_specs=[pl.BlockSpec((tm, k), lambda i, j: (i, 0)),
                  pl.BlockSpec((k, tn), lambda i, j: (0, j))],
        out_specs=pl.BlockSpec((tm, tn), lambda i, j: (i, j)),
        compiler_params=_cparams(("parallel", "arbitrary")),
        name=name,
    )(a, b)


class _WeightStream:
    def __init__(self, w_hbm, wb_ref, stage_ref, sem, layer, tn):
        self.w_hbm, self.wb_ref, self.stage_ref, self.sem = w_hbm, wb_ref, stage_ref, sem
        self.layer, self.tn = layer, tn
        self.j, self.i = pl.program_id(0), pl.program_id(1)
        self.nj, self.ni = pl.num_programs(0), pl.num_programs(1)
        self.chunk_rows = stage_ref.shape[1]

    def _rows(self, chunk):
        return pl.ds(pl.multiple_of(chunk * self.chunk_rows, self.chunk_rows), self.chunk_rows)

    def _copy(self, block, chunk):
        cols = pl.ds(pl.multiple_of(block * self.tn, self.tn), self.tn)
        src = self.w_hbm.at[self.layer, self._rows(chunk), cols]
        slot = chunk % 2
        return pltpu.make_async_copy(src, self.stage_ref.at[slot], self.sem.at[slot])

    def _round(self, wb_slot, chunk):
        self.wb_ref[wb_slot, self._rows(chunk), :] = self.stage_ref[chunk % 2].astype(BF16)

    def _prefetch_block(self, j):
        return jnp.minimum(j + 1, self.nj - 1)

    def before_matmul(self, n_chunks):
        j, i = self.j, self.i
        step = j * self.ni + i

        @pl.when(step == 0)
        def _():
            self._copy(0, 0).start()
            for c in range(n_chunks):
                if c + 1 < n_chunks:
                    self._copy(0, c + 1).start()
                self._copy(0, c).wait()
                self._round(0, c)
            self._copy(self._prefetch_block(0), 0).start()

        self._copy(self._prefetch_block(j), i).wait()

        @pl.when(step + 1 < self.nj * self.ni)
        def _():
            wrap = i + 1 == self.ni
            i_next = jnp.where(wrap, 0, i + 1)
            j_next = jnp.where(wrap, j + 1, j)
            self._copy(self._prefetch_block(j_next), i_next).start()

        self._round((j + 1) % 2, i)
        return self.wb_ref.at[j % 2]


def _mm_wstream_kernel(a_ref, w_hbm, o_ref, wb_ref, stage_ref, sem, *, layer, n_chunks):
    stream = _WeightStream(w_hbm, wb_ref, stage_ref, sem, layer, o_ref.shape[1])
    w = stream.before_matmul(n_chunks)
    o_ref[...] = jnp.dot(a_ref[...], w[...], preferred_element_type=F32).astype(o_ref.dtype)


def _mm_wstream_residual_kernel(a_ref, w_hbm, r_ref, o_ref, wb_ref, stage_ref, sem, *, layer, n_chunks):
    stream = _WeightStream(w_hbm, wb_ref, stage_ref, sem, layer, o_ref.shape[1])
    w = stream.before_matmul(n_chunks)
    acc = jnp.dot(a_ref[...], w[...], preferred_element_type=F32)
    o_ref[...] = (r_ref[...] + acc).astype(o_ref.dtype)


def _weight_stream_scratch(k, tn, n_chunks):
    return [pltpu.VMEM((2, k, tn), BF16),
            pltpu.VMEM((2, k // n_chunks, tn), F32),
            pltpu.SemaphoreType.DMA((2,))]


def matmul_f32_weights(a, w_stack, layer, out_dtype, tm, tn, residual=None, name="matmul_w"):
    m, k = a.shape
    n = w_stack.shape[-1]
    n_chunks = m // tm
    in_specs = [pl.BlockSpec((tm, k), lambda j, i: (i, 0)),
                pl.BlockSpec(memory_space=pl.ANY)]
    args = [a, w_stack]
    body = _mm_wstream_kernel
    if residual is not None:
        in_specs.append(pl.BlockSpec((tm, tn), lambda j, i: (i, j)))
        args.append(residual)
        body = _mm_wstream_residual_kernel
    return pl.pallas_call(
        functools.partial(body, layer=layer, n_chunks=n_chunks),
        out_shape=jax.ShapeDtypeStruct((m, n), out_dtype),
        grid=(n // tn, n_chunks),
        in_specs=in_specs,
        out_specs=pl.BlockSpec((tm, tn), lambda j, i: (i, j)),
        scratch_shapes=_weight_stream_scratch(k, tn, n_chunks),
        compiler_params=_cparams(("arbitrary", "arbitrary")),
        name=name,
    )(*args)


MOBA_TQ = MOBA_BLOCK


def _moba_kernel(slopes_ref, q_ref, k_ref, v_ref, z_ref, o_ref, qa_ref, ka_ref, va_ref):
    h = pl.program_id(1)
    nb, blk, hd = N_MOBA_BLOCKS, MOBA_BLOCK, HEAD_DIM

    k = k_ref[...]
    ka_ref[:, :hd] = k
    blk_shift = blk.bit_length() - 1
    key_blk = lax.broadcasted_iota(jnp.int32, (SEQ, hd), 0) >> blk_shift
    ka_ref[:, hd:] = jnp.where(key_blk == lax.broadcasted_iota(jnp.int32, (SEQ, hd), 1), 1.0, 0.0).astype(BF16)
    va_ref[:, :hd] = v_ref[...]
    va_ref[:, hd:] = jnp.ones((SEQ, hd), BF16)

    km = jnp.concatenate(
        [jnp.mean(k[n * blk:(n + 1) * blk, :].astype(F32), axis=0, keepdims=True) for n in range(nb)], axis=0)
    q = q_ref[...]
    gate_t = lax.dot_general(km, q.astype(F32), NT_DIMS, precision=lax.Precision.HIGHEST,
                             preferred_element_type=F32)
    n_idx = lax.broadcasted_iota(jnp.int32, (nb, SEQ), 0)
    q_blk = lax.broadcasted_iota(jnp.int32, (nb, SEQ), 1) >> blk_shift
    past = n_idx < q_blk
    g = jnp.where(past, gate_t, NEG)
    rank = jnp.zeros((nb, SEQ), jnp.int32)
    for m in range(nb):
        row = g[m:m + 1, :]
        rank = rank + jnp.where(row > g, 1, jnp.where(row == g, jnp.where(n_idx > m, 1, 0), 0))
    mask_t = jnp.where(past, jnp.where(rank < MOBA_TOPK, 0.0, NEG),
                       jnp.where(n_idx == q_blk, 0.0, NEG))
    mask_rows = jnp.concatenate([mask_t, jnp.zeros((hd - nb, SEQ), F32)], axis=0).T
    qa_ref[:, :hd] = q
    qa_ref[:, hd:] = mask_rows.astype(BF16)

    slope2 = slopes_ref[B_HEADS + h] * LOG2E
    c1 = (hd ** -0.5) * LOG2E
    r_idx = lax.broadcasted_iota(jnp.int32, (MOBA_TQ, blk), 0)
    c_idx = lax.broadcasted_iota(jnp.int32, (MOBA_TQ, blk), 1)
    causal = r_idx >= c_idx
    def raw_scores(i):
        return lax.dot_general(qa_ref[i * MOBA_TQ:(i + 1) * MOBA_TQ, :], ka_ref[:(i + 1) * blk, :], NT_DIMS,
                               preferred_element_type=F32)

    s_next = raw_scores(0)
    for i in range(nb):
        rows = slice(i * MOBA_TQ, (i + 1) * MOBA_TQ)
        width = (i + 1) * blk
        s = s_next
        if i + 1 < nb:
            s_next = raw_scores(i + 1)
        key_pos = lax.broadcasted_iota(jnp.int32, (1, width), 1) - i * blk
        t = s * c1 + slope2 * key_pos.astype(F32)
        t_own = jnp.where(causal, t[:, i * blk:], NEG)
        t = t_own if i == 0 else jnp.concatenate([t[:, :i * blk], t_own], axis=1)
        p = jnp.exp2(t - jnp.max(t, axis=-1, keepdims=True))
        oa = jnp.dot(p.astype(BF16), va_ref[:width, :], preferred_element_type=F32)
        out = oa[:, :hd] / oa[:, hd:]
        o_ref[rows, :] = (out * jax.nn.silu(z_ref[rows, :].astype(F32))).astype(o_ref.dtype)


def moba_branch(proj, slopes):
    hpb = BW // HEAD_DIM
    seg = lambda c: pl.BlockSpec((SEQ, HEAD_DIM), lambda b, h: (b, c * hpb + h))
    return pl.pallas_call(
        _moba_kernel,
        out_shape=jax.ShapeDtypeStruct((ROWS, BW), BF16),
        grid=(BATCH, A_HEADS),
        in_specs=[pl.BlockSpec(memory_space=pltpu.SMEM), seg(COL_QA), seg(COL_KA), seg(COL_VA), seg(COL_ZA)],
        out_specs=pl.BlockSpec((SEQ, HEAD_DIM), lambda b, h: (b, h)),
        scratch_shapes=[pltpu.VMEM((SEQ, 2 * HEAD_DIM), BF16),
                        pltpu.VMEM((SEQ, 2 * HEAD_DIM), BF16),
                        pltpu.VMEM((SEQ, 2 * HEAD_DIM), BF16)],
        compiler_params=_cparams(("parallel", "parallel")),
        name="moba_branch",
    )(slopes, proj, proj, proj, proj)


DIFF_TQ = 256


def _diff_kernel(slopes_ref, lq1_ref, lk1_ref, lq2_ref, lk2_ref, subg_ref, q_ref, k_ref, v_ref, z_ref,
                 o_ref, *, lam_init):
    h = pl.program_id(1)
    tq, hd = DIFF_TQ, HEAD_DIM
    slope2 = slopes_ref[h] * LOG2E
    c1 = (hd ** -0.5) * LOG2E
    lam = (jnp.exp(jnp.sum(lq1_ref[...] * lk1_ref[...], axis=-1, keepdims=True))
           - jnp.exp(jnp.sum(lq2_ref[...] * lk2_ref[...], axis=-1, keepdims=True)) + lam_init)
    r_idx = lax.broadcasted_iota(jnp.int32, (tq, tq), 0)
    c_idx = lax.broadcasted_iota(jnp.int32, (tq, tq), 1)
    causal = r_idx >= c_idx
    n_tiles = SEQ // tq

    def raw_scores(i, mp):
        cols = slice(mp * hd, (mp + 1) * hd)
        return lax.dot_general(q_ref[i * tq:(i + 1) * tq, cols], k_ref[:(i + 1) * tq, cols], NT_DIMS,
                               preferred_element_type=F32)

    def softmax_numerator(s, i):
        key_bias = slope2 * (lax.broadcasted_iota(jnp.int32, (1, (i + 1) * tq), 1) - i * tq).astype(F32)
        t = s * c1 + key_bias
        t_own = jnp.where(causal, t[:, i * tq:], NEG)
        t = t_own if i == 0 else jnp.concatenate([t[:, :i * tq], t_own], axis=1)
        p = jnp.exp2(t - jnp.max(t, axis=-1, keepdims=True))
        return p.astype(BF16), jnp.sum(p, axis=-1, keepdims=True)

    s0_next = raw_scores(0, 0)
    for i in range(n_tiles):
        rows = slice(i * tq, (i + 1) * tq)
        s0, s1 = s0_next, raw_scores(i, 1)
        p0, l0 = softmax_numerator(s0, i)
        if i + 1 < n_tiles:
            s0_next = raw_scores(i + 1, 0)
        p1, l1 = softmax_numerator(s1, i)
        o = jnp.dot(jnp.concatenate([p0, p1], axis=0), v_ref[:(i + 1) * tq, :], preferred_element_type=F32)
        ob = o[:tq] / l0 - lam * (o[tq:] / l1)
        ms = jnp.mean(ob * ob, axis=-1, keepdims=True)
        ob = ob * lax.rsqrt(ms + RMS_EPS) * subg_ref[...]
        ob = ob * (1.0 - lam_init)
        o_ref[rows, :] = (ob * jax.nn.silu(z_ref[rows, :].astype(F32))).astype(o_ref.dtype)


def diff_branch(proj, slopes, lq1, lk1, lq2, lk2, subln_g, lam_init):
    hpb = BW // B_V_DIM
    vec = lambda a: a.reshape(1, -1).astype(F32)
    small = lambda n: pl.BlockSpec((1, n), lambda b, h: (0, 0))
    seg = lambda c: pl.BlockSpec((SEQ, B_V_DIM), lambda b, h: (b, c * hpb + h))
    return pl.pallas_call(
        functools.partial(_diff_kernel, lam_init=lam_init),
        out_shape=jax.ShapeDtypeStruct((ROWS, BW), BF16),
        grid=(BATCH, B_HEADS),
        in_specs=[pl.BlockSpec(memory_space=pltpu.SMEM),
                  small(HEAD_DIM), small(HEAD_DIM), small(HEAD_DIM), small(HEAD_DIM), small(B_V_DIM),
                  seg(COL_QB), seg(COL_KB), seg(COL_VB), seg(COL_ZB)],
        out_specs=pl.BlockSpec((SEQ, B_V_DIM), lambda b, h: (b, h)),
        compiler_params=_cparams(("parallel", "parallel")),
        name="diff_branch",
    )(slopes, vec(lq1), vec(lk1), vec(lq2), vec(lk2), vec(subln_g), proj, proj, proj, proj)


SGU_TILE_CHUNKS = 4


def _sgu_kernel(u_ref, v_ref, z_ref, lng_ref, lnb_ref, w_ref, bt_ref, o_ref):
    v = jax.nn.gelu(v_ref[...].astype(F32))
    mu = jnp.mean(v, axis=-1, keepdims=True)
    var = jnp.mean(jnp.square(v - mu), axis=-1, keepdims=True)
    vn = ((v - mu) * lax.rsqrt(var + LN_EPS) * lng_ref[...] + lnb_ref[...]).astype(BF16)
    r_idx = lax.broadcasted_iota(jnp.int32, (SGU_CHUNK, SGU_CHUNK), 0)
    c_idx = lax.broadcasted_iota(jnp.int32, (SGU_CHUNK, SGU_CHUNK), 1)
    causal = r_idx >= c_idx
    chunk_rows = [slice(c * SGU_CHUNK, (c + 1) * SGU_CHUNK) for c in range(SGU_TILE_CHUNKS)]
    for g in range(SGU_GROUPS):
        cols = slice(g * SGU_GROUP_CH, (g + 1) * SGU_GROUP_CH)
        w = jnp.where(causal, w_ref[g], 0.0).astype(BF16)
        vn_g = jnp.concatenate([vn[rows, cols] for rows in chunk_rows], axis=1)
        mixed_g = jnp.dot(w, vn_g, preferred_element_type=F32) + bt_ref[:, g:g + 1]
        for c, rows in enumerate(chunk_rows):
            mixed = mixed_g[:, c * SGU_GROUP_CH:(c + 1) * SGU_GROUP_CH]
            u = jax.nn.gelu(u_ref[rows, cols].astype(F32))
            o_ref[rows, cols] = (u * mixed * jax.nn.silu(z_ref[rows, cols].astype(F32))).astype(o_ref.dtype)


def sgu_branch(proj, ln_g, ln_b, w_s, b_s):
    tile = SGU_TILE_CHUNKS * SGU_CHUNK
    seg = lambda c: pl.BlockSpec((tile, BW), lambda t: (t, c))
    return pl.pallas_call(
        _sgu_kernel,
        out_shape=jax.ShapeDtypeStruct((ROWS, BW), BF16),
        grid=(ROWS // tile,),
        in_specs=[seg(COL_UC), seg(COL_VC), seg(COL_ZC),
                  pl.BlockSpec((1, BW), lambda t: (0, 0)),
                  pl.BlockSpec((1, BW), lambda t: (0, 0)),
                  pl.BlockSpec((SGU_GROUPS, SGU_CHUNK, SGU_CHUNK), lambda t: (0, 0, 0)),
                  pl.BlockSpec((SGU_CHUNK, SGU_GROUPS), lambda t: (0, 0))],
        out_specs=pl.BlockSpec((tile, BW), lambda t: (t, 0)),
        compiler_params=_cparams(("parallel",)),
        name="sgu_branch",
    )(proj, proj, proj, ln_g.reshape(1, BW), ln_b.reshape(1, BW), w_s, b_s.T)


MEM_TQ = 512


def _mem_kernel(q_ref, kv_ref, z_ref, o_ref):
    c1 = (M_HEAD_DIM ** -0.5) * LOG2E
    for h in range(M_HEADS):
        cols = slice(h * M_HEAD_DIM, (h + 1) * M_HEAD_DIM)
        k = kv_ref[:, cols]
        v = kv_ref[:, BW + h * M_HEAD_DIM:BW + (h + 1) * M_HEAD_DIM]
        for i in range(SEQ // MEM_TQ):
            rows = slice(i * MEM_TQ, (i + 1) * MEM_TQ)
            t = lax.dot_general(q_ref[rows, cols], k, NT_DIMS, preferred_element_type=F32) * c1
            p = jnp.exp2(t - jnp.max(t, axis=-1, keepdims=True))
            l = jnp.sum(p, axis=-1, keepdims=True)
            out = jnp.dot(p.astype(BF16), v, preferred_element_type=F32) / l
            o_ref[rows, cols] = (out * jax.nn.silu(z_ref[rows, cols].astype(F32))).astype(o_ref.dtype)


def mem_branch(proj, mem_kv):
    return pl.pallas_call(
        _mem_kernel,
        out_shape=jax.ShapeDtypeStruct((ROWS, BW), BF16),
        grid=(BATCH,),
        in_specs=[pl.BlockSpec((SEQ, BW), lambda b: (b, COL_QM)),
                  pl.BlockSpec((MEM_LEN, 2 * BW), lambda b: (b, 0)),
                  pl.BlockSpec((SEQ, BW), lambda b: (b, COL_ZM))],
        out_specs=pl.BlockSpec((SEQ, BW), lambda b: (b, 0)),
        compiler_params=_cparams(("parallel",)),
        name="mem_branch",
    )(proj, mem_kv, proj)


MERGE_TM = 512
MERGE_TN = 1024


def _merge_kernel(ya_ref, yb_ref, yc_ref, ym_ref, w_hbm, g0_ref, g1_ref, g2_ref, g3_ref, o_ref,
                  wb_ref, stage_ref, sem, *, layer, n_chunks):
    stream = _WeightStream(w_hbm, wb_ref, stage_ref, sem, layer, o_ref.shape[1])
    w = stream.before_matmul(n_chunks)
    ys = (ya_ref, yb_ref, yc_ref, ym_ref)
    gs = (g0_ref, g1_ref, g2_ref, g3_ref)
    merged = None
    for br in range(N_BRANCH):
        gate = jax.nn.sigmoid(gs[br][...].astype(F32))
        term = gate * jnp.dot(ys[br][...], w[br * BW:(br + 1) * BW, :], preferred_element_type=F32)
        merged = term if merged is None else merged + term
    o_ref[...] = merged.astype(o_ref.dtype)


def gated_merge(ya, yb, yc, ym, w_branch, layer, proj):
    tm, tn = MERGE_TM, MERGE_TN
    n_chunks = ROWS // tm
    k = N_BRANCH * BW
    y_spec = pl.BlockSpec((tm, BW), lambda j, i: (i, 0))
    gate_spec = lambda br: pl.BlockSpec(
        (tm, tn), lambda j, i: (i, (COL_GATE * BW + br * D_MODEL) // tn + j))
    return pl.pallas_call(
        functools.partial(_merge_kernel, layer=layer, n_chunks=n_chunks),
        out_shape=jax.ShapeDtypeStruct((ROWS, D_MODEL), BF16),
        grid=(D_MODEL // tn, n_chunks),
        in_specs=[y_spec, y_spec, y_spec, y_spec,
                  pl.BlockSpec(memory_space=pl.ANY),
                  gate_spec(0), gate_spec(1), gate_spec(2), gate_spec(3)],
        out_specs=pl.BlockSpec((tm, tn), lambda j, i: (i, j)),
        scratch_shapes=_weight_stream_scratch(k, tn, n_chunks),
        compiler_params=_cparams(("arbitrary", "arbitrary")),
        name="gated_merge",
    )(ya, yb, yc, ym, w_branch.reshape(DEPTH, k, D_MODEL), proj, proj, proj, proj)


def kernel(x, mem, norm_g, w_in, mem_norm_g, w_mem_kv, diff_lam_q1, diff_lam_k1, diff_lam_q2, diff_lam_k2,
           diff_subln_g, sgu_ln_g, sgu_ln_b, sgu_w, sgu_b, w_branch, w_out, final_g):
    xf = x.reshape(ROWS, D_MODEL)
    memf = mem.reshape(BATCH * MEM_LEN, D_MODEL)
    slopes = jnp.exp2(-8.0 * jnp.arange(1, N_ALIBI_HEADS + 1, dtype=F32) / N_ALIBI_HEADS)
    for l in range(DEPTH):
        lam_init = 0.8 - 0.6 * math.exp(-0.3 * l)
        h = rmsnorm(xf, norm_g[l], BF16)
        proj = matmul_f32_weights(h, w_in, l, BF16, tm=1024, tn=1024, name="in_proj")
        hm = rmsnorm(memf, mem_norm_g[l], BF16)
        mem_kv = matmul(hm, w_mem_kv[l].astype(BF16), BF16, tm=1024, tn=1024, name="mem_kv_proj")
        ya = moba_branch(proj, slopes)
        yb = diff_branch(proj, slopes, diff_lam_q1[l], diff_lam_k1[l], diff_lam_q2[l], diff_lam_k2[l],
                         diff_subln_g[l], lam_init)
        yc = sgu_branch(proj, sgu_ln_g[l], sgu_ln_b[l], sgu_w[l], sgu_b[l])
        ym = mem_branch(proj, mem_kv)
        merged = gated_merge(ya, yb, yc, ym, w_branch, l, proj)
        xf = matmul_f32_weights(merged, w_out, l, F32, tm=512, tn=1024, residual=xf, name="out_proj")
    out = rmsnorm(xf, final_g, F32)
    return out.reshape(BATCH, SEQ, D_MODEL)
```

```python
import functools
import math

import jax
import jax.numpy as jnp
from jax import lax
from jax.experimental import pallas as pl
from jax.experimental.pallas import tpu as pltpu

D_MODEL = 4096
BATCH = 4
SEQ = 2048
DEPTH = 2
MEM_LEN = 256
HEAD_DIM = 128
BW = D_MODEL // 4
N_BRANCH = 4
A_HEADS = BW // HEAD_DIM
MOBA_BLOCK = 256
MOBA_TOPK = 3
B_HEADS = BW // (2 * HEAD_DIM)
B_V_DIM = 2 * HEAD_DIM
SGU_CHUNK = 128
SGU_GROUP_CH = 128
SGU_GROUPS = BW // SGU_GROUP_CH
M_HEADS = 4
M_HEAD_DIM = BW // M_HEADS
N_ALIBI_HEADS = A_HEADS + B_HEADS
RMS_EPS = 1e-6
LN_EPS = 1e-5
NEG = -1e30
W_IN_COLS = 13 * BW + N_BRANCH * D_MODEL
N_MOBA_BLOCKS = SEQ // MOBA_BLOCK

(COL_QA, COL_KA, COL_VA, COL_ZA, COL_QB, COL_KB, COL_VB, COL_ZB,
 COL_UC, COL_VC, COL_ZC, COL_QM, COL_ZM, COL_GATE) = range(14)

ROWS = BATCH * SEQ
F32 = jnp.float32
BF16 = jnp.bfloat16
LOG2E = math.log2(math.e)

VMEM_LIMIT_BYTES = 58 * 1024 * 1024
SUBLANES = 8

NT_DIMS = (((1,), (1,)), ((), ()))


def _cparams(semantics):
    return pltpu.CompilerParams(dimension_semantics=semantics, vmem_limit_bytes=VMEM_LIMIT_BYTES)


def _rmsnorm_kernel(x_ref, g_ref, o_ref):
    x = x_ref[...].astype(F32)
    ms = jnp.mean(x * x, axis=-1, keepdims=True)
    o_ref[...] = (x * lax.rsqrt(ms + RMS_EPS) * g_ref[...]).astype(o_ref.dtype)


def rmsnorm(x, g, out_dtype, tile_rows=512):
    rows, d = x.shape
    return pl.pallas_call(
        _rmsnorm_kernel,
        out_shape=jax.ShapeDtypeStruct((rows, d), out_dtype),
        grid=(rows // tile_rows,),
        in_specs=[pl.BlockSpec((tile_rows, d), lambda i: (i, 0)),
                  pl.BlockSpec((1, d), lambda i: (0, 0))],
        out_specs=pl.BlockSpec((tile_rows, d), lambda i: (i, 0)),
        compiler_params=_cparams(("parallel",)),
        name="rmsnorm",
    )(x, g.reshape(1, d).astype(F32))


class _WeightStream:
    def __init__(self, w_hbm, wb_ref, stage_ref, sem, layer, tn):
        self.w_hbm, self.wb_ref, self.stage_ref, self.sem = w_hbm, wb_ref, stage_ref, sem
        self.layer, self.tn = layer, tn
        self.j, self.i = pl.program_id(0), pl.program_id(1)
        self.nj, self.ni = pl.num_programs(0), pl.num_programs(1)
        self.chunk_rows = stage_ref.shape[1]

    def _rows(self, chunk):
        return pl.ds(pl.multiple_of(chunk * self.chunk_rows, self.chunk_rows), self.chunk_rows)

    def _copy(self, block, chunk):
        cols = pl.ds(pl.multiple_of(block * self.tn, self.tn), self.tn)
        src = self.w_hbm.at[self.layer, self._rows(chunk), cols]
        slot = chunk % 2
        return pltpu.make_async_copy(src, self.stage_ref.at[slot], self.sem.at[slot])

    def _round(self, wb_slot, chunk):
        self.wb_ref[wb_slot, self._rows(chunk), :] = self.stage_ref[chunk % 2].astype(BF16)

    def _prefetch_block(self, j):
        return jnp.minimum(j + 1, self.nj - 1)

    def before_matmul(self, n_chunks):
        j, i = self.j, self.i
        step = j * self.ni + i

        @pl.when(step == 0)
        def _():
            self._copy(0, 0).start()
            for c in range(n_chunks):
                if c + 1 < n_chunks:
                    self._copy(0, c + 1).start()
                self._copy(0, c).wait()
                self._round(0, c)
            self._copy(self._prefetch_block(0), 0).start()

        self._copy(self._prefetch_block(j), i).wait()

        @pl.when(step + 1 < self.nj * self.ni)
        def _():
            wrap = i + 1 == self.ni
            i_next = jnp.where(wrap, 0, i + 1)
            j_next = jnp.where(wrap, j + 1, j)
            self._copy(self._prefetch_block(j_next), i_next).start()

        self._round((j + 1) % 2, i)
        return self.wb_ref.at[j % 2]


def _mm_wstream_kernel(a_ref, w_hbm, o_ref, wb_ref, stage_ref, sem, *, layer, n_chunks):
    stream = _WeightStream(w_hbm, wb_ref, stage_ref, sem, layer, o_ref.shape[1])
    w = stream.before_matmul(n_chunks)
    o_ref[...] = jnp.dot(a_ref[...], w[...], preferred_element_type=F32).astype(o_ref.dtype)


def _mm_wstream_residual_kernel(a_ref, w_hbm, r_ref, o_ref, wb_ref, stage_ref, sem, *, layer, n_chunks):
    stream = _WeightStream(w_hbm, wb_ref, stage_ref, sem, layer, o_ref.shape[1])
    w = stream.before_matmul(n_chunks)
    acc = jnp.dot(a_ref[...], w[...], preferred_element_type=F32)
    o_ref[...] = (r_ref[...] + acc).astype(o_ref.dtype)


def _weight_stream_scratch(k, tn, n_chunks):
    return [pltpu.VMEM((2, k, tn), BF16),
            pltpu.VMEM((2, k // n_chunks, tn), F32),
            pltpu.SemaphoreType.DMA((2,))]


def matmul_f32_weights(a, w_stack, layer, out_dtype, tm, tn, residual=None, name="matmul_w"):
    m, k = a.shape
    n = w_stack.shape[-1]
    n_chunks = m // tm
    in_specs = [pl.BlockSpec((tm, k), lambda j, i: (i, 0)),
                pl.BlockSpec(memory_space=pl.ANY)]
    args = [a, w_stack]
    body = _mm_wstream_kernel
    if residual is not None:
        in_specs.append(pl.BlockSpec((tm, tn), lambda j, i: (i, j)))
        args.append(residual)
        body = _mm_wstream_residual_kernel
    return pl.pallas_call(
        functools.partial(body, layer=layer, n_chunks=n_chunks),
        out_shape=jax.ShapeDtypeStruct((m, n), out_dtype),
        grid=(n // tn, n_chunks),
        in_specs=in_specs,
        out_specs=pl.BlockSpec((tm, tn), lambda j, i: (i, j)),
        scratch_shapes=_weight_stream_scratch(k, tn, n_chunks),
        compiler_params=_cparams(("arbitrary", "arbitrary")),
        name=name,
    )(*args)


MOBA_TQ = MOBA_BLOCK


ALIBI_TERMS = 3


def _alibi_key_columns(slope, first_col, base):
    pos = lax.broadcasted_iota(jnp.int32, (SEQ, HEAD_DIM), 0) - SEQ // 2
    lane = lax.broadcasted_iota(jnp.int32, (SEQ, HEAD_DIM), 1)
    rest = (slope * HEAD_DIM ** 0.5) * pos.astype(F32)
    cols = base
    for term in range(ALIBI_TERMS):
        piece = rest.astype(BF16).astype(F32)
        rest = rest - piece
        cols = jnp.where(lane == first_col + term, piece, cols)
    return cols


def _alibi_query_columns(first_col, base):
    lane = lax.broadcasted_iota(jnp.int32, base.shape, 1)
    return jnp.where(lane < first_col, base, jnp.where(lane < first_col + ALIBI_TERMS, 1.0, base))


def _moba_kernel(slopes_ref, q_ref, k_ref, v_ref, z_ref, o_ref, qa_ref, ka_ref, va_ref):
    h = pl.program_id(1)
    nb, blk, hd = N_MOBA_BLOCKS, MOBA_BLOCK, HEAD_DIM

    k = k_ref[...]
    ka_ref[:, :hd] = k
    blk_shift = blk.bit_length() - 1
    key_blk = lax.broadcasted_iota(jnp.int32, (SEQ, hd), 0) >> blk_shift
    block_onehot = jnp.where(key_blk == lax.broadcasted_iota(jnp.int32, (SEQ, hd), 1), 1.0, 0.0)
    ka_ref[:, hd:] = _alibi_key_columns(slopes_ref[B_HEADS + h], nb, block_onehot).astype(BF16)
    va_ref[:, :hd] = v_ref[...]
    va_ref[:, hd:] = jnp.ones((SEQ, hd), BF16)

    km = jnp.concatenate(
        [jnp.mean(k[n * blk:(n + 1) * blk, :].astype(F32), axis=0, keepdims=True) for n in range(nb)], axis=0)
    q = q_ref[...]
    gate_t = lax.dot_general(km, q.astype(F32), NT_DIMS, precision=lax.Precision.HIGHEST,
                             preferred_element_type=F32)
    n_idx = lax.broadcasted_iota(jnp.int32, (nb, SEQ), 0)
    q_blk = lax.broadcasted_iota(jnp.int32, (nb, SEQ), 1) >> blk_shift
    past = n_idx < q_blk
    g = jnp.where(past, gate_t, NEG)
    rank = jnp.zeros((nb, SEQ), jnp.int32)
    for m in range(nb):
        row = g[m:m + 1, :]
        rank = rank + jnp.where(row > g, 1, jnp.where(row == g, jnp.where(n_idx > m, 1, 0), 0))
    mask_t = jnp.where(past, jnp.where(rank < MOBA_TOPK, 0.0, NEG),
                       jnp.where(n_idx == q_blk, 0.0, NEG))
    mask_rows = jnp.concatenate([mask_t, jnp.zeros((hd - nb, SEQ), F32)], axis=0).T
    qa_ref[:, :hd] = q
    qa_ref[:, hd:] = _alibi_query_columns(nb, mask_rows).astype(BF16)

    c1 = (hd ** -0.5) * LOG2E
    r_idx = lax.broadcasted_iota(jnp.int32, (MOBA_TQ, blk), 0)
    c_idx = lax.broadcasted_iota(jnp.int32, (MOBA_TQ, blk), 1)
    causal = r_idx >= c_idx

    def raw_scores(i):
        return lax.dot_general(qa_ref[i * MOBA_TQ:(i + 1) * MOBA_TQ, :], ka_ref[:(i + 1) * blk, :], NT_DIMS,
                               preferred_element_type=F32)

    s_next = raw_scores(0)
    for i in range(nb):
        rows = slice(i * MOBA_TQ, (i + 1) * MOBA_TQ)
        width = (i + 1) * blk
        s = s_next
        if i + 1 < nb:
            s_next = raw_scores(i + 1)
        s_own = jnp.where(causal, s[:, i * blk:], NEG)
        s = s_own if i == 0 else jnp.concatenate([s[:, :i * blk], s_own], axis=1)
        p = jnp.exp2((s - jnp.max(s, axis=-1, keepdims=True)) * c1)
        oa = jnp.dot(p.astype(BF16), va_ref[:width, :], preferred_element_type=F32)
        out = oa[:, :hd] / oa[:, hd:]
        o_ref[rows, :] = (out * jax.nn.silu(z_ref[rows, :].astype(F32))).astype(o_ref.dtype)


def moba_branch(proj, slopes):
    hpb = BW // HEAD_DIM
    seg = lambda c: pl.BlockSpec((SEQ, HEAD_DIM), lambda b, h: (b, c * hpb + h))
    return pl.pallas_call(
        _moba_kernel,
        out_shape=jax.ShapeDtypeStruct((ROWS, BW), BF16),
        grid=(BATCH, A_HEADS),
        in_specs=[pl.BlockSpec(memory_space=pltpu.SMEM), seg(COL_QA), seg(COL_KA), seg(COL_VA), seg(COL_ZA)],
        out_specs=pl.BlockSpec((SEQ, HEAD_DIM), lambda b, h: (b, h)),
        scratch_shapes=[pltpu.VMEM((SEQ, 2 * HEAD_DIM), BF16),
                        pltpu.VMEM((SEQ, 2 * HEAD_DIM), BF16),
                        pltpu.VMEM((SEQ, 2 * HEAD_DIM), BF16)],
        compiler_params=_cparams(("parallel", "parallel")),
        name="moba_branch",
    )(slopes, proj, proj, proj, proj)


DIFF_TQ = 256


def _diff_kernel(slopes_ref, lq1_ref, lk1_ref, lq2_ref, lk2_ref, subg_ref, q_ref, k_ref, v_ref, z_ref,
                 o_ref, qa_ref, ka_ref, *, lam_init):
    h = pl.program_id(1)
    tq, hd = DIFF_TQ, HEAD_DIM
    c1 = (hd ** -0.5) * LOG2E
    zeros = jnp.zeros((SEQ, hd), F32)
    q_aug = _alibi_query_columns(0, zeros).astype(BF16)
    k_aug = _alibi_key_columns(slopes_ref[h], 0, zeros).astype(BF16)
    for mp in range(2):
        qa_ref[:, 2 * mp * hd:(2 * mp + 1) * hd] = q_ref[:, mp * hd:(mp + 1) * hd]
        qa_ref[:, (2 * mp + 1) * hd:(2 * mp + 2) * hd] = q_aug
        ka_ref[:, 2 * mp * hd:(2 * mp + 1) * hd] = k_ref[:, mp * hd:(mp + 1) * hd]
        ka_ref[:, (2 * mp + 1) * hd:(2 * mp + 2) * hd] = k_aug
    lam = (jnp.exp(jnp.sum(lq1_ref[...] * lk1_ref[...], axis=-1, keepdims=True))
           - jnp.exp(jnp.sum(lq2_ref[...] * lk2_ref[...], axis=-1, keepdims=True)) + lam_init)
    r_idx = lax.broadcasted_iota(jnp.int32, (tq, tq), 0)
    c_idx = lax.broadcasted_iota(jnp.int32, (tq, tq), 1)
    causal = r_idx >= c_idx
    n_tiles = SEQ // tq

    def raw_scores(i, mp):
        cols = slice(2 * mp * hd, (2 * mp + 2) * hd)
        return lax.dot_general(qa_ref[i * tq:(i + 1) * tq, cols], ka_ref[:(i + 1) * tq, cols], NT_DIMS,
                               preferred_element_type=F32)

    def softmax_numerator(s, i):
        s_own = jnp.where(causal, s[:, i * tq:], NEG)
        s = s_own if i == 0 else jnp.concatenate([s[:, :i * tq], s_own], axis=1)
        p = jnp.exp2((s - jnp.max(s, axis=-1, keepdims=True)) * c1)
        return p.astype(BF16), jnp.sum(p, axis=-1, keepdims=True)

    s0_next = raw_scores(0, 0)
    for i in range(n_tiles):
        rows = slice(i * tq, (i + 1) * tq)
        s0, s1 = s0_next, raw_scores(i, 1)
        p0, l0 = softmax_numerator(s0, i)
        if i + 1 < n_tiles:
            s0_next = raw_scores(i + 1, 0)
        p1, l1 = softmax_numerator(s1, i)
        o = jnp.dot(jnp.concatenate([p0, p1], axis=0), v_ref[:(i + 1) * tq, :], preferred_element_type=F32)
        ob = o[:tq] / l0 - lam * (o[tq:] / l1)
        ms = jnp.mean(ob * ob, axis=-1, keepdims=True)
        ob = ob * lax.rsqrt(ms + RMS_EPS) * subg_ref[...]
        ob = ob * (1.0 - lam_init)
        o_ref[rows, :] = (ob * jax.nn.silu(z_ref[rows, :].astype(F32))).astype(o_ref.dtype)


def diff_branch(proj, slopes, lq1, lk1, lq2, lk2, subln_g, lam_init):
    hpb = BW // B_V_DIM
    vec = lambda a: a.reshape(1, -1).astype(F32)
    small = lambda n: pl.BlockSpec((1, n), lambda b, h: (0, 0))
    seg = lambda c: pl.BlockSpec((SEQ, B_V_DIM), lambda b, h: (b, c * hpb + h))
    return pl.pallas_call(
        functools.partial(_diff_kernel, lam_init=lam_init),
        out_shape=jax.ShapeDtypeStruct((ROWS, BW), BF16),
        grid=(BATCH, B_HEADS),
        in_specs=[pl.BlockSpec(memory_space=pltpu.SMEM),
                  small(HEAD_DIM), small(HEAD_DIM), small(HEAD_DIM), small(HEAD_DIM), small(B_V_DIM),
                  seg(COL_QB), seg(COL_KB), seg(COL_VB), seg(COL_ZB)],
        out_specs=pl.BlockSpec((SEQ, B_V_DIM), lambda b, h: (b, h)),
        scratch_shapes=[pltpu.VMEM((SEQ, 4 * HEAD_DIM), BF16), pltpu.VMEM((SEQ, 4 * HEAD_DIM), BF16)],
        compiler_params=_cparams(("parallel", "parallel")),
        name="diff_branch",
    )(slopes, vec(lq1), vec(lk1), vec(lq2), vec(lk2), vec(subln_g), proj, proj, proj, proj)


SGU_TILE_CHUNKS = 4


def _sgu_kernel(u_ref, v_ref, z_ref, lng_ref, lnb_ref, w_ref, bt_ref, o_ref):
    v = jax.nn.gelu(v_ref[...].astype(F32))
    mu = jnp.mean(v, axis=-1, keepdims=True)
    var = jnp.mean(jnp.square(v - mu), axis=-1, keepdims=True)
    vn = ((v - mu) * lax.rsqrt(var + LN_EPS) * lng_ref[...] + lnb_ref[...]).astype(BF16)
    r_idx = lax.broadcasted_iota(jnp.int32, (SGU_CHUNK, SGU_CHUNK), 0)
    c_idx = lax.broadcasted_iota(jnp.int32, (SGU_CHUNK, SGU_CHUNK), 1)
    causal = r_idx >= c_idx
    chunk_rows = [slice(c * SGU_CHUNK, (c + 1) * SGU_CHUNK) for c in range(SGU_TILE_CHUNKS)]
    for g in range(SGU_GROUPS):
        cols = slice(g * SGU_GROUP_CH, (g + 1) * SGU_GROUP_CH)
        w = jnp.where(causal, w_ref[g], 0.0).astype(BF16)
        vn_g = jnp.concatenate([vn[rows, cols] for rows in chunk_rows], axis=1)
        mixed_g = jnp.dot(w, vn_g, preferred_element_type=F32) + bt_ref[:, g:g + 1]
        for c, rows in enumerate(chunk_rows):
            mixed = mixed_g[:, c * SGU_GROUP_CH:(c + 1) * SGU_GROUP_CH]
            u = jax.nn.gelu(u_ref[rows, cols].astype(F32))
            o_ref[rows, cols] = (u * mixed * jax.nn.silu(z_ref[rows, cols].astype(F32))).astype(o_ref.dtype)


def sgu_branch(proj, ln_g, ln_b, w_s, b_s):
    tile = SGU_TILE_CHUNKS * SGU_CHUNK
    seg = lambda c: pl.BlockSpec((tile, BW), lambda t: (t, c))
    return pl.pallas_call(
        _sgu_kernel,
        out_shape=jax.ShapeDtypeStruct((ROWS, BW), BF16),
        grid=(ROWS // tile,),
        in_specs=[seg(COL_UC), seg(COL_VC), seg(COL_ZC),
                  pl.BlockSpec((1, BW), lambda t: (0, 0)),
                  pl.BlockSpec((1, BW), lambda t: (0, 0)),
                  pl.BlockSpec((SGU_GROUPS, SGU_CHUNK, SGU_CHUNK), lambda t: (0, 0, 0)),
                  pl.BlockSpec((SGU_CHUNK, SGU_GROUPS), lambda t: (0, 0))],
        out_specs=pl.BlockSpec((tile, BW), lambda t: (t, 0)),
        compiler_params=_cparams(("parallel",)),
        name="sgu_branch",
    )(proj, proj, proj, ln_g.reshape(1, BW), ln_b.reshape(1, BW), w_s, b_s.T)


MEM_TQ = 512


def _mem_kernel(q_ref, kv_ref, z_ref, o_ref):
    c1 = (M_HEAD_DIM ** -0.5) * LOG2E
    for h in range(M_HEADS):
        cols = slice(h * M_HEAD_DIM, (h + 1) * M_HEAD_DIM)
        k = kv_ref[:, cols]
        v = kv_ref[:, BW + h * M_HEAD_DIM:BW + (h + 1) * M_HEAD_DIM]
        for i in range(SEQ // MEM_TQ):
            rows = slice(i * MEM_TQ, (i + 1) * MEM_TQ)
            t = lax.dot_general(q_ref[rows, cols], k, NT_DIMS, preferred_element_type=F32) * c1
            p = jnp.exp2(t - jnp.max(t, axis=-1, keepdims=True))
            l = jnp.sum(p, axis=-1, keepdims=True)
            out = jnp.dot(p.astype(BF16), v, preferred_element_type=F32) / l
            o_ref[rows, cols] = (out * jax.nn.silu(z_ref[rows, cols].astype(F32))).astype(o_ref.dtype)


def mem_branch(proj, mem_kv):
    return pl.pallas_call(
        _mem_kernel,
        out_shape=jax.ShapeDtypeStruct((ROWS, BW), BF16),
        grid=(BATCH,),
        in_specs=[pl.BlockSpec((SEQ, BW), lambda b: (b, COL_QM)),
                  pl.BlockSpec((MEM_LEN, 2 * BW), lambda b: (b, 0)),
                  pl.BlockSpec((SEQ, BW), lambda b: (b, COL_ZM))],
        out_specs=pl.BlockSpec((SEQ, BW), lambda b: (b, 0)),
        compiler_params=_cparams(("parallel",)),
        name="mem_branch",
    )(proj, mem_kv, proj)


MERGE_TM = 512
MERGE_TN = 1024


def _merge_kernel(ya_ref, yb_ref, yc_ref, ym_ref, w_hbm, g0_ref, g1_ref, g2_ref, g3_ref, o_ref,
                  wb_ref, stage_ref, sem, *, layer, n_chunks):
    stream = _WeightStream(w_hbm, wb_ref, stage_ref, sem, layer, o_ref.shape[1])
    w = stream.before_matmul(n_chunks)
    ys = (ya_ref, yb_ref, yc_ref, ym_ref)
    gs = (g0_ref, g1_ref, g2_ref, g3_ref)
    merged = None
    for br in range(N_BRANCH):
        gate = jax.nn.sigmoid(gs[br][...].astype(F32))
        term = gate * jnp.dot(ys[br][...], w[br * BW:(br + 1) * BW, :], preferred_element_type=F32)
        merged = term if merged is None else merged + term
    o_ref[...] = merged.astype(o_ref.dtype)


def gated_merge(ya, yb, yc, ym, w_branch, layer, proj):
    tm, tn = MERGE_TM, MERGE_TN
    n_chunks = ROWS // tm
    k = N_BRANCH * BW
    y_spec = pl.BlockSpec((tm, BW), lambda j, i: (i, 0))
    gate_spec = lambda br: pl.BlockSpec(
        (tm, tn), lambda j, i: (i, (COL_GATE * BW + br * D_MODEL) // tn + j))
    return pl.pallas_call(
        functools.partial(_merge_kernel, layer=layer, n_chunks=n_chunks),
        out_shape=jax.ShapeDtypeStruct((ROWS, D_MODEL), BF16),
        grid=(D_MODEL // tn, n_chunks),
        in_specs=[y_spec, y_spec, y_spec, y_spec,
                  pl.BlockSpec(memory_space=pl.ANY),
                  gate_spec(0), gate_spec(1), gate_spec(2), gate_spec(3)],
        out_specs=pl.BlockSpec((tm, tn), lambda j, i: (i, j)),
        scratch_shapes=_weight_stream_scratch(k, tn, n_chunks),
        compiler_params=_cparams(("arbitrary", "arbitrary")),
        name="gated_merge",
    )(ya, yb, yc, ym, w_branch.reshape(DEPTH, k, D_MODEL), proj, proj, proj, proj)


def kernel(x, mem, norm_g, w_in, mem_norm_g, w_mem_kv, diff_lam_q1, diff_lam_k1, diff_lam_q2, diff_lam_k2,
           diff_subln_g, sgu_ln_g, sgu_ln_b, sgu_w, sgu_b, w_branch, w_out, final_g):
    xf = x.reshape(ROWS, D_MODEL)
    memf = mem.reshape(BATCH * MEM_LEN, D_MODEL)
    slopes = jnp.exp2(-8.0 * jnp.arange(1, N_ALIBI_HEADS + 1, dtype=F32) / N_ALIBI_HEADS)
    for l in range(DEPTH):
        lam_init = 0.8 - 0.6 * math.exp(-0.3 * l)
        h = rmsnorm(xf, norm_g[l], BF16)
        proj = matmul_f32_weights(h, w_in, l, BF16, tm=1024, tn=1024, name="in_proj")
        hm = rmsnorm(memf, mem_norm_g[l], BF16)
        mem_kv = matmul_f32_weights(hm, w_mem_kv, l, BF16, tm=256, tn=1024, name="mem_kv_proj")
        ya = moba_branch(proj, slopes)
        yb = diff_branch(proj, slopes, diff_lam_q1[l], diff_lam_k1[l], diff_lam_q2[l], diff_lam_k2[l],
                         diff_subln_g[l], lam_init)
        yc = sgu_branch(proj, sgu_ln_g[l], sgu_ln_b[l], sgu_w[l], sgu_b[l])
        ym = mem_branch(proj, mem_kv)
        merged = gated_merge(ya, yb, yc, ym, w_branch, l, proj)
        xf = matmul_f32_weights(merged, w_out, l, F32, tm=512, tn=1024, residual=xf, name="out_proj")
    out = rmsnorm(xf, final_g, F32)
    return out.reshape(BATCH, SEQ, D_MODEL)
```

```python
import functools
import math

import jax
import jax.numpy as jnp
from jax import lax
from jax.experimental import pallas as pl
from jax.experimental.pallas import tpu as pltpu

D_MODEL = 4096
BATCH = 4
SEQ = 2048
DEPTH = 2
MEM_LEN = 256
HEAD_DIM = 128
BW = D_MODEL // 4
N_BRANCH = 4
A_HEADS = BW // HEAD_DIM
MOBA_BLOCK = 256
MOBA_TOPK = 3
B_HEADS = BW // (2 * HEAD_DIM)
B_V_DIM = 2 * HEAD_DIM
SGU_CHUNK = 128
SGU_GROUP_CH = 128
SGU_GROUPS = BW // SGU_GROUP_CH
M_HEADS = 4
M_HEAD_DIM = BW // M_HEADS
N_ALIBI_HEADS = A_HEADS + B_HEADS
RMS_EPS = 1e-6
LN_EPS = 1e-5
NEG = -1e30
W_IN_COLS = 13 * BW + N_BRANCH * D_MODEL
N_MOBA_BLOCKS = SEQ // MOBA_BLOCK

(COL_QA, COL_KA, COL_VA, COL_ZA, COL_QB, COL_KB, COL_VB, COL_ZB,
 COL_UC, COL_VC, COL_ZC, COL_QM, COL_ZM, COL_GATE) = range(14)

ROWS = BATCH * SEQ
F32 = jnp.float32
BF16 = jnp.bfloat16
LOG2E = math.log2(math.e)

VMEM_LIMIT_BYTES = 58 * 1024 * 1024
SUBLANES = 8

NT_DIMS = (((1,), (1,)), ((), ()))


def _cparams(semantics):
    return pltpu.CompilerParams(dimension_semantics=semantics, vmem_limit_bytes=VMEM_LIMIT_BYTES)


def _rmsnorm_kernel(x_ref, g_ref, o_ref):
    x = x_ref[...].astype(F32)
    ms = jnp.mean(x * x, axis=-1, keepdims=True)
    o_ref[...] = (x * lax.rsqrt(ms + RMS_EPS) * g_ref[...]).astype(o_ref.dtype)


def rmsnorm(x, g, out_dtype, tile_rows=512):
    rows, d = x.shape
    return pl.pallas_call(
        _rmsnorm_kernel,
        out_shape=jax.ShapeDtypeStruct((rows, d), out_dtype),
        grid=(rows // tile_rows,),
        in_specs=[pl.BlockSpec((tile_rows, d), lambda i: (i, 0)),
                  pl.BlockSpec((1, d), lambda i: (0, 0))],
        out_specs=pl.BlockSpec((tile_rows, d), lambda i: (i, 0)),
        compiler_params=_cparams(("parallel",)),
        name="rmsnorm",
    )(x, g.reshape(1, d).astype(F32))


STAGE_SLOTS = 4


class _WeightStream:
    def __init__(self, w_hbm, wb_ref, stage_ref, sem, layer, tn):
        self.w_hbm, self.wb_ref, self.stage_ref, self.sem = w_hbm, wb_ref, stage_ref, sem
        self.layer, self.tn = layer, tn
        self.j, self.i = pl.program_id(0), pl.program_id(1)
        self.nj, self.ni = pl.num_programs(0), pl.num_programs(1)
        self.chunk_rows = stage_ref.shape[1]

    def _rows(self, chunk):
        return pl.ds(pl.multiple_of(chunk * self.chunk_rows, self.chunk_rows), self.chunk_rows)

    def _copy(self, block, chunk):
        cols = pl.ds(pl.multiple_of(block * self.tn, self.tn), self.tn)
        src = self.w_hbm.at[self.layer, self._rows(chunk), cols]
        slot = chunk % STAGE_SLOTS
        return pltpu.make_async_copy(src, self.stage_ref.at[slot], self.sem.at[slot])

    def _round(self, wb_slot, chunk):
        self.wb_ref[wb_slot, self._rows(chunk), :] = self.stage_ref[chunk % STAGE_SLOTS].astype(BF16)

    def _prefetch_block(self, j):
        return jnp.minimum(j + 1, self.nj - 1)

    def before_matmul(self, n_chunks):
        j, i = self.j, self.i
        step = j * self.ni + i

        @pl.when(step == 0)
        def _():
            depth = STAGE_SLOTS - 1
            for c in range(depth):
                self._copy(0, c).start()
            for c in range(n_chunks):
                if c + depth < n_chunks:
                    self._copy(0, c + depth).start()
                self._copy(0, c).wait()
                self._round(0, c)
            self._copy(self._prefetch_block(0), 0).start()

        self._copy(self._prefetch_block(j), i).wait()

        @pl.when(step + 1 < self.nj * self.ni)
        def _():
            wrap = i + 1 == self.ni
            i_next = jnp.where(wrap, 0, i + 1)
            j_next = jnp.where(wrap, j + 1, j)
            self._copy(self._prefetch_block(j_next), i_next).start()

        self._round((j + 1) % 2, i)
        return self.wb_ref.at[j % 2]


def _mm_wstream_kernel(a_ref, w_hbm, o_ref, wb_ref, stage_ref, sem, *, layer, n_chunks):
    stream = _WeightStream(w_hbm, wb_ref, stage_ref, sem, layer, o_ref.shape[1])
    w = stream.before_matmul(n_chunks)
    o_ref[...] = jnp.dot(a_ref[...], w[...], preferred_element_type=F32).astype(o_ref.dtype)


def _mm_wstream_residual_kernel(a_ref, w_hbm, r_ref, o_ref, wb_ref, stage_ref, sem, *, layer, n_chunks):
    stream = _WeightStream(w_hbm, wb_ref, stage_ref, sem, layer, o_ref.shape[1])
    w = stream.before_matmul(n_chunks)
    acc = jnp.dot(a_ref[...], w[...], preferred_element_type=F32)
    o_ref[...] = (r_ref[...] + acc).astype(o_ref.dtype)


def _weight_stream_scratch(k, tn, n_chunks):
    assert n_chunks % STAGE_SLOTS == 0
    return [pltpu.VMEM((2, k, tn), BF16),
            pltpu.VMEM((STAGE_SLOTS, k // n_chunks, tn), F32),
            pltpu.SemaphoreType.DMA((STAGE_SLOTS,))]


def matmul_f32_weights(a, w_stack, layer, out_dtype, tm, tn, residual=None, name="matmul_w"):
    m, k = a.shape
    n = w_stack.shape[-1]
    n_chunks = m // tm
    in_specs = [pl.BlockSpec((tm, k), lambda j, i: (i, 0)),
                pl.BlockSpec(memory_space=pl.ANY)]
    args = [a, w_stack]
    body = _mm_wstream_kernel
    if residual is not None:
        in_specs.append(pl.BlockSpec((tm, tn), lambda j, i: (i, j)))
        args.append(residual)
        body = _mm_wstream_residual_kernel
    return pl.pallas_call(
        functools.partial(body, layer=layer, n_chunks=n_chunks),
        out_shape=jax.ShapeDtypeStruct((m, n), out_dtype),
        grid=(n // tn, n_chunks),
        in_specs=in_specs,
        out_specs=pl.BlockSpec((tm, tn), lambda j, i: (i, j)),
        scratch_shapes=_weight_stream_scratch(k, tn, n_chunks),
        compiler_params=_cparams(("arbitrary", "arbitrary")),
        name=name,
    )(*args)


MOBA_TQ = MOBA_BLOCK


ALIBI_TERMS = 3


def _alibi_key_columns(slope, first_col, base):
    pos = lax.broadcasted_iota(jnp.int32, (SEQ, HEAD_DIM), 0) - SEQ // 2
    lane = lax.broadcasted_iota(jnp.int32, (SEQ, HEAD_DIM), 1)
    rest = (slope * HEAD_DIM ** 0.5) * pos.astype(F32)
    cols = base
    for term in range(ALIBI_TERMS):
        piece = rest.astype(BF16).astype(F32)
        rest = rest - piece
        cols = jnp.where(lane == first_col + term, piece, cols)
    return cols


def _alibi_query_columns(first_col, base):
    lane = lax.broadcasted_iota(jnp.int32, base.shape, 1)
    return jnp.where(lane < first_col, base, jnp.where(lane < first_col + ALIBI_TERMS, 1.0, base))


def _moba_kernel(slopes_ref, q_ref, k_ref, v_ref, z_ref, o_ref, qa_ref, ka_ref, va_ref):
    h = pl.program_id(1)
    nb, blk, hd = N_MOBA_BLOCKS, MOBA_BLOCK, HEAD_DIM

    k = k_ref[...]
    ka_ref[:, :hd] = k
    blk_shift = blk.bit_length() - 1
    key_blk = lax.broadcasted_iota(jnp.int32, (SEQ, hd), 0) >> blk_shift
    block_onehot = jnp.where(key_blk == lax.broadcasted_iota(jnp.int32, (SEQ, hd), 1), 1.0, 0.0)
    ka_ref[:, hd:] = _alibi_key_columns(slopes_ref[B_HEADS + h], nb, block_onehot).astype(BF16)
    va_ref[:, :hd] = v_ref[...]
    va_ref[:, hd:] = jnp.ones((SEQ, hd), BF16)

    km = jnp.concatenate(
        [jnp.mean(k[n * blk:(n + 1) * blk, :].astype(F32), axis=0, keepdims=True) for n in range(nb)], axis=0)
    q = q_ref[...]
    gate_t = lax.dot_general(km, q.astype(F32), NT_DIMS, precision=lax.Precision.HIGHEST,
                             preferred_element_type=F32)
    n_idx = lax.broadcasted_iota(jnp.int32, (nb, SEQ), 0)
    q_blk = lax.broadcasted_iota(jnp.int32, (nb, SEQ), 1) >> blk_shift
    past = n_idx < q_blk
    g = jnp.where(past, gate_t, NEG)
    rank = jnp.zeros((nb, SEQ), jnp.int32)
    for m in range(nb):
        row = g[m:m + 1, :]
        rank = rank + jnp.where(row > g, 1, jnp.where(row == g, jnp.where(n_idx > m, 1, 0), 0))
    mask_t = jnp.where(past, jnp.where(rank < MOBA_TOPK, 0.0, NEG),
                       jnp.where(n_idx == q_blk, 0.0, NEG))
    mask_rows = jnp.concatenate([mask_t, jnp.zeros((hd - nb, SEQ), F32)], axis=0).T
    qa_ref[:, :hd] = q
    qa_ref[:, hd:] = _alibi_query_columns(nb, mask_rows).astype(BF16)

    c1 = (hd ** -0.5) * LOG2E
    r_idx = lax.broadcasted_iota(jnp.int32, (MOBA_TQ, blk), 0)
    c_idx = lax.broadcasted_iota(jnp.int32, (MOBA_TQ, blk), 1)
    causal = r_idx >= c_idx

    def raw_scores(i):
        return lax.dot_general(qa_ref[i * MOBA_TQ:(i + 1) * MOBA_TQ, :], ka_ref[:(i + 1) * blk, :], NT_DIMS,
                               preferred_element_type=F32)

    s_next = raw_scores(0)
    for i in range(nb):
        rows = slice(i * MOBA_TQ, (i + 1) * MOBA_TQ)
        width = (i + 1) * blk
        s = s_next
        if i + 1 < nb:
            s_next = raw_scores(i + 1)
        s_own = jnp.where(causal, s[:, i * blk:], NEG)
        s = s_own if i == 0 else jnp.concatenate([s[:, :i * blk], s_own], axis=1)
        p = jnp.exp2((s - jnp.max(s, axis=-1, keepdims=True)) * c1)
        oa = jnp.dot(p.astype(BF16), va_ref[:width, :], preferred_element_type=F32)
        out = oa[:, :hd] / oa[:, hd:]
        o_ref[rows, :] = (out * jax.nn.silu(z_ref[rows, :].astype(F32))).astype(o_ref.dtype)


def moba_branch(proj, slopes):
    hpb = BW // HEAD_DIM
    seg = lambda c: pl.BlockSpec((SEQ, HEAD_DIM), lambda b, h: (b, c * hpb + h))
    return pl.pallas_call(
        _moba_kernel,
        out_shape=jax.ShapeDtypeStruct((ROWS, BW), BF16),
        grid=(BATCH, A_HEADS),
        in_specs=[pl.BlockSpec(memory_space=pltpu.SMEM), seg(COL_QA), seg(COL_KA), seg(COL_VA), seg(COL_ZA)],
        out_specs=pl.BlockSpec((SEQ, HEAD_DIM), lambda b, h: (b, h)),
        scratch_shapes=[pltpu.VMEM((SEQ, 2 * HEAD_DIM), BF16),
                        pltpu.VMEM((SEQ, 2 * HEAD_DIM), BF16),
                        pltpu.VMEM((SEQ, 2 * HEAD_DIM), BF16)],
        compiler_params=_cparams(("parallel", "parallel")),
        name="moba_branch",
    )(slopes, proj, proj, proj, proj)


DIFF_TQ = 256


def _diff_kernel(slopes_ref, lq1_ref, lk1_ref, lq2_ref, lk2_ref, subg_ref, q_ref, k_ref, v_ref, z_ref,
                 o_ref, qa_ref, ka_ref, *, lam_init):
    h = pl.program_id(1)
    tq, hd = DIFF_TQ, HEAD_DIM
    c1 = (hd ** -0.5) * LOG2E
    zeros = jnp.zeros((SEQ, hd), F32)
    q_aug = _alibi_query_columns(0, zeros).astype(BF16)
    k_aug = _alibi_key_columns(slopes_ref[h], 0, zeros).astype(BF16)
    for mp in range(2):
        qa_ref[:, 2 * mp * hd:(2 * mp + 1) * hd] = q_ref[:, mp * hd:(mp + 1) * hd]
        qa_ref[:, (2 * mp + 1) * hd:(2 * mp + 2) * hd] = q_aug
        ka_ref[:, 2 * mp * hd:(2 * mp + 1) * hd] = k_ref[:, mp * hd:(mp + 1) * hd]
        ka_ref[:, (2 * mp + 1) * hd:(2 * mp + 2) * hd] = k_aug
    lam = (jnp.exp(jnp.sum(lq1_ref[...] * lk1_ref[...], axis=-1, keepdims=True))
           - jnp.exp(jnp.sum(lq2_ref[...] * lk2_ref[...], axis=-1, keepdims=True)) + lam_init)
    r_idx = lax.broadcasted_iota(jnp.int32, (tq, tq), 0)
    c_idx = lax.broadcasted_iota(jnp.int32, (tq, tq), 1)
    causal = r_idx >= c_idx
    n_tiles = SEQ // tq

    def raw_scores(i, mp):
        cols = slice(2 * mp * hd, (2 * mp + 2) * hd)
        return lax.dot_general(qa_ref[i * tq:(i + 1) * tq, cols], ka_ref[:(i + 1) * tq, cols], NT_DIMS,
                               preferred_element_type=F32)

    def softmax_numerator(s, i):
        s_own = jnp.where(causal, s[:, i * tq:], NEG)
        s = s_own if i == 0 else jnp.concatenate([s[:, :i * tq], s_own], axis=1)
        p = jnp.exp2((s - jnp.max(s, axis=-1, keepdims=True)) * c1)
        return p.astype(BF16), jnp.sum(p, axis=-1, keepdims=True)

    s0_next = raw_scores(0, 0)
    for i in range(n_tiles):
        rows = slice(i * tq, (i + 1) * tq)
        s0, s1 = s0_next, raw_scores(i, 1)
        p0, l0 = softmax_numerator(s0, i)
        if i + 1 < n_tiles:
            s0_next = raw_scores(i + 1, 0)
        p1, l1 = softmax_numerator(s1, i)
        o = jnp.dot(jnp.concatenate([p0, p1], axis=0), v_ref[:(i + 1) * tq, :], preferred_element_type=F32)
        ob = o[:tq] / l0 - lam * (o[tq:] / l1)
        ms = jnp.mean(ob * ob, axis=-1, keepdims=True)
        ob = ob * lax.rsqrt(ms + RMS_EPS) * subg_ref[...]
        ob = ob * (1.0 - lam_init)
        o_ref[rows, :] = (ob * jax.nn.silu(z_ref[rows, :].astype(F32))).astype(o_ref.dtype)


def diff_branch(proj, slopes, lq1, lk1, lq2, lk2, subln_g, lam_init):
    hpb = BW // B_V_DIM
    vec = lambda a: a.reshape(1, -1).astype(F32)
    small = lambda n: pl.BlockSpec((1, n), lambda b, h: (0, 0))
    seg = lambda c: pl.BlockSpec((SEQ, B_V_DIM), lambda b, h: (b, c * hpb + h))
    return pl.pallas_call(
        functools.partial(_diff_kernel, lam_init=lam_init),
        out_shape=jax.ShapeDtypeStruct((ROWS, BW), BF16),
        grid=(BATCH, B_HEADS),
        in_specs=[pl.BlockSpec(memory_space=pltpu.SMEM),
                  small(HEAD_DIM), small(HEAD_DIM), small(HEAD_DIM), small(HEAD_DIM), small(B_V_DIM),
                  seg(COL_QB), seg(COL_KB), seg(COL_VB), seg(COL_ZB)],
        out_specs=pl.BlockSpec((SEQ, B_V_DIM), lambda b, h: (b, h)),
        scratch_shapes=[pltpu.VMEM((SEQ, 4 * HEAD_DIM), BF16), pltpu.VMEM((SEQ, 4 * HEAD_DIM), BF16)],
        compiler_params=_cparams(("parallel", "parallel")),
        name="diff_branch",
    )(slopes, vec(lq1), vec(lk1), vec(lq2), vec(lk2), vec(subln_g), proj, proj, proj, proj)


SGU_TILE_CHUNKS = 4


def _sgu_kernel(u_ref, v_ref, z_ref, lng_ref, lnb_ref, w_ref, bt_ref, o_ref):
    v = jax.nn.gelu(v_ref[...].astype(F32))
    mu = jnp.mean(v, axis=-1, keepdims=True)
    var = jnp.mean(jnp.square(v - mu), axis=-1, keepdims=True)
    vn = ((v - mu) * lax.rsqrt(var + LN_EPS) * lng_ref[...] + lnb_ref[...]).astype(BF16)
    r_idx = lax.broadcasted_iota(jnp.int32, (SGU_CHUNK, SGU_CHUNK), 0)
    c_idx = lax.broadcasted_iota(jnp.int32, (SGU_CHUNK, SGU_CHUNK), 1)
    causal = r_idx >= c_idx
    chunk_rows = [slice(c * SGU_CHUNK, (c + 1) * SGU_CHUNK) for c in range(SGU_TILE_CHUNKS)]
    for g in range(SGU_GROUPS):
        cols = slice(g * SGU_GROUP_CH, (g + 1) * SGU_GROUP_CH)
        w = jnp.where(causal, w_ref[g], 0.0).astype(BF16)
        vn_g = jnp.concatenate([vn[rows, cols] for rows in chunk_rows], axis=1)
        mixed_g = jnp.dot(w, vn_g, preferred_element_type=F32) + bt_ref[:, g:g + 1]
        for c, rows in enumerate(chunk_rows):
            mixed = mixed_g[:, c * SGU_GROUP_CH:(c + 1) * SGU_GROUP_CH]
            u = jax.nn.gelu(u_ref[rows, cols].astype(F32))
            o_ref[rows, cols] = (u * mixed * jax.nn.silu(z_ref[rows, cols].astype(F32))).astype(o_ref.dtype)


def sgu_branch(proj, ln_g, ln_b, w_s, b_s):
    tile = SGU_TILE_CHUNKS * SGU_CHUNK
    seg = lambda c: pl.BlockSpec((tile, BW), lambda t: (t, c))
    return pl.pallas_call(
        _sgu_kernel,
        out_shape=jax.ShapeDtypeStruct((ROWS, BW), BF16),
        grid=(ROWS // tile,),
        in_specs=[seg(COL_UC), seg(COL_VC), seg(COL_ZC),
                  pl.BlockSpec((1, BW), lambda t: (0, 0)),
                  pl.BlockSpec((1, BW), lambda t: (0, 0)),
                  pl.BlockSpec((SGU_GROUPS, SGU_CHUNK, SGU_CHUNK), lambda t: (0, 0, 0)),
                  pl.BlockSpec((SGU_CHUNK, SGU_GROUPS), lambda t: (0, 0))],
        out_specs=pl.BlockSpec((tile, BW), lambda t: (t, 0)),
        compiler_params=_cparams(("parallel",)),
        name="sgu_branch",
    )(proj, proj, proj, ln_g.reshape(1, BW), ln_b.reshape(1, BW), w_s, b_s.T)


MEM_TQ = 512


def _mem_kernel(q_ref, kv_ref, z_ref, o_ref):
    c1 = (M_HEAD_DIM ** -0.5) * LOG2E
    for h in range(M_HEADS):
        cols = slice(h * M_HEAD_DIM, (h + 1) * M_HEAD_DIM)
        k = kv_ref[:, cols]
        v = kv_ref[:, BW + h * M_HEAD_DIM:BW + (h + 1) * M_HEAD_DIM]
        for i in range(SEQ // MEM_TQ):
            rows = slice(i * MEM_TQ, (i + 1) * MEM_TQ)
            t = lax.dot_general(q_ref[rows, cols], k, NT_DIMS, preferred_element_type=F32) * c1
            p = jnp.exp2(t - jnp.max(t, axis=-1, keepdims=True))
            l = jnp.sum(p, axis=-1, keepdims=True)
            out = jnp.dot(p.astype(BF16), v, preferred_element_type=F32) / l
            o_ref[rows, cols] = (out * jax.nn.silu(z_ref[rows, cols].astype(F32))).astype(o_ref.dtype)


def mem_branch(proj, mem_kv):
    return pl.pallas_call(
        _mem_kernel,
        out_shape=jax.ShapeDtypeStruct((ROWS, BW), BF16),
        grid=(BATCH,),
        in_specs=[pl.BlockSpec((SEQ, BW), lambda b: (b, COL_QM)),
                  pl.BlockSpec((MEM_LEN, 2 * BW), lambda b: (b, 0)),
                  pl.BlockSpec((SEQ, BW), lambda b: (b, COL_ZM))],
        out_specs=pl.BlockSpec((SEQ, BW), lambda b: (b, 0)),
        compiler_params=_cparams(("parallel",)),
        name="mem_branch",
    )(proj, mem_kv, proj)


MERGE_TM = 512
MERGE_TN = 1024


def _merge_kernel(ya_ref, yb_ref, yc_ref, ym_ref, w_hbm, g0_ref, g1_ref, g2_ref, g3_ref, o_ref,
                  wb_ref, stage_ref, sem, *, layer, n_chunks):
    stream = _WeightStream(w_hbm, wb_ref, stage_ref, sem, layer, o_ref.shape[1])
    w = stream.before_matmul(n_chunks)
    ys = (ya_ref, yb_ref, yc_ref, ym_ref)
    gs = (g0_ref, g1_ref, g2_ref, g3_ref)
    merged = None
    for br in range(N_BRANCH):
        gate = jax.nn.sigmoid(gs[br][...].astype(F32))
        term = gate * jnp.dot(ys[br][...], w[br * BW:(br + 1) * BW, :], preferred_element_type=F32)
        merged = term if merged is None else merged + term
    o_ref[...] = merged.astype(o_ref.dtype)


def gated_merge(ya, yb, yc, ym, w_branch, layer, proj):
    tm, tn = MERGE_TM, MERGE_TN
    n_chunks = ROWS // tm
    k = N_BRANCH * BW
    y_spec = pl.BlockSpec((tm, BW), lambda j, i: (i, 0))
    gate_spec = lambda br: pl.BlockSpec(
        (tm, tn), lambda j, i: (i, (COL_GATE * BW + br * D_MODEL) // tn + j))
    return pl.pallas_call(
        functools.partial(_merge_kernel, layer=layer, n_chunks=n_chunks),
        out_shape=jax.ShapeDtypeStruct((ROWS, D_MODEL), BF16),
        grid=(D_MODEL // tn, n_chunks),
        in_specs=[y_spec, y_spec, y_spec, y_spec,
                  pl.BlockSpec(memory_space=pl.ANY),
                  gate_spec(0), gate_spec(1), gate_spec(2), gate_spec(3)],
        out_specs=pl.BlockSpec((tm, tn), lambda j, i: (i, j)),
        scratch_shapes=_weight_stream_scratch(k, tn, n_chunks),
        compiler_params=_cparams(("arbitrary", "arbitrary")),
        name="gated_merge",
    )(ya, yb, yc, ym, w_branch.reshape(DEPTH, k, D_MODEL), proj, proj, proj, proj)


def kernel(x, mem, norm_g, w_in, mem_norm_g, w_mem_kv, diff_lam_q1, diff_lam_k1, diff_lam_q2, diff_lam_k2,
           diff_subln_g, sgu_ln_g, sgu_ln_b, sgu_w, sgu_b, w_branch, w_out, final_g):
    xf = x.reshape(ROWS, D_MODEL)
    memf = mem.reshape(BATCH * MEM_LEN, D_MODEL)
    slopes = jnp.exp2(-8.0 * jnp.arange(1, N_ALIBI_HEADS + 1, dtype=F32) / N_ALIBI_HEADS)
    for l in range(DEPTH):
        lam_init = 0.8 - 0.6 * math.exp(-0.3 * l)
        h = rmsnorm(xf, norm_g[l], BF16)
        proj = matmul_f32_weights(h, w_in, l, BF16, tm=1024, tn=1024, name="in_proj")
        hm = rmsnorm(memf, mem_norm_g[l], BF16)
        mem_kv = matmul_f32_weights(hm, w_mem_kv, l, BF16, tm=256, tn=1024, name="mem_kv_proj")
        ya = moba_branch(proj, slopes)
        yb = diff_branch(proj, slopes, diff_lam_q1[l], diff_lam_k1[l], diff_lam_q2[l], diff_lam_k2[l],
                         diff_subln_g[l], lam_init)
        yc = sgu_branch(proj, sgu_ln_g[l], sgu_ln_b[l], sgu_w[l], sgu_b[l])
        ym = mem_branch(proj, mem_kv)
        merged = gated_merge(ya, yb, yc, ym, w_branch, l, proj)
        xf = matmul_f32_weights(merged, w_out, l, F32, tm=512, tn=1024, residual=xf, name="out_proj")
    out = rmsnorm(xf, final_g, F32)
    return out.reshape(BATCH, SEQ, D_MODEL)
```

```python
import functools
import math

import jax
import jax.numpy as jnp
from jax import lax
from jax.experimental import pallas as pl
from jax.experimental.pallas import tpu as pltpu

D_MODEL = 4096
BATCH = 4
SEQ = 2048
DEPTH = 2
MEM_LEN = 256
HEAD_DIM = 128
BW = D_MODEL // 4
N_BRANCH = 4
A_HEADS = BW // HEAD_DIM
MOBA_BLOCK = 256
MOBA_TOPK = 3
B_HEADS = BW // (2 * HEAD_DIM)
B_V_DIM = 2 * HEAD_DIM
SGU_CHUNK = 128
SGU_GROUP_CH = 128
SGU_GROUPS = BW // SGU_GROUP_CH
M_HEADS = 4
M_HEAD_DIM = BW // M_HEADS
N_ALIBI_HEADS = A_HEADS + B_HEADS
RMS_EPS = 1e-6
LN_EPS = 1e-5
NEG = -1e30
N_MOBA_BLOCKS = SEQ // MOBA_BLOCK

(COL_QA, COL_KA, COL_VA, COL_ZA, COL_QB, COL_KB, COL_VB, COL_ZB,
 COL_UC, COL_VC, COL_ZC, COL_QM, COL_ZM, COL_GATE) = range(14)

ROWS = BATCH * SEQ
F32 = jnp.float32
BF16 = jnp.bfloat16
LOG2E = math.log2(math.e)

VMEM_LIMIT_BYTES = 58 * 1024 * 1024

NT_DIMS = (((1,), (1,)), ((), ()))


def _cparams(semantics):
    return pltpu.CompilerParams(dimension_semantics=semantics, vmem_limit_bytes=VMEM_LIMIT_BYTES)


def _rmsnorm_kernel(x_ref, g_ref, o_ref):
    x = x_ref[...].astype(F32)
    ms = jnp.mean(x * x, axis=-1, keepdims=True)
    o_ref[...] = (x * lax.rsqrt(ms + RMS_EPS) * g_ref[...]).astype(o_ref.dtype)


def rmsnorm(x, g, out_dtype, tile_rows=512):
    rows, d = x.shape
    return pl.pallas_call(
        _rmsnorm_kernel,
        out_shape=jax.ShapeDtypeStruct((rows, d), out_dtype),
        grid=(rows // tile_rows,),
        in_specs=[pl.BlockSpec((tile_rows, d), lambda i: (i, 0)),
                  pl.BlockSpec((1, d), lambda i: (0, 0))],
        out_specs=pl.BlockSpec((tile_rows, d), lambda i: (i, 0)),
        compiler_params=_cparams(("parallel",)),
        name="rmsnorm",
    )(x, g.reshape(1, d).astype(F32))


STAGE_SLOTS = 4


class _WeightStream:
    def __init__(self, w_hbm, wb_ref, stage_ref, sem, layer, tn):
        self.w_hbm, self.wb_ref, self.stage_ref, self.sem = w_hbm, wb_ref, stage_ref, sem
        self.layer, self.tn = layer, tn
        self.j, self.i = pl.program_id(0), pl.program_id(1)
        self.nj, self.ni = pl.num_programs(0), pl.num_programs(1)
        self.chunk_rows = stage_ref.shape[1]

    def _rows(self, chunk):
        return pl.ds(pl.multiple_of(chunk * self.chunk_rows, self.chunk_rows), self.chunk_rows)

    def _copy(self, block, chunk):
        cols = pl.ds(pl.multiple_of(block * self.tn, self.tn), self.tn)
        src = self.w_hbm.at[self.layer, self._rows(chunk), cols]
        slot = chunk % STAGE_SLOTS
        return pltpu.make_async_copy(src, self.stage_ref.at[slot], self.sem.at[slot])

    def _round(self, wb_slot, chunk):
        self.wb_ref[wb_slot, self._rows(chunk), :] = self.stage_ref[chunk % STAGE_SLOTS].astype(BF16)

    def _prefetch_block(self, j):
        return jnp.minimum(j + 1, self.nj - 1)

    def before_matmul(self, n_chunks):
        j, i = self.j, self.i
        step = j * self.ni + i

        @pl.when(step == 0)
        def _():
            depth = STAGE_SLOTS - 1
            for c in range(depth):
                self._copy(0, c).start()
            for c in range(n_chunks):
                if c + depth < n_chunks:
                    self._copy(0, c + depth).start()
                self._copy(0, c).wait()
                self._round(0, c)
            self._copy(self._prefetch_block(0), 0).start()

        self._copy(self._prefetch_block(j), i).wait()

        @pl.when(step + 1 < self.nj * self.ni)
        def _():
            wrap = i + 1 == self.ni
            i_next = jnp.where(wrap, 0, i + 1)
            j_next = jnp.where(wrap, j + 1, j)
            self._copy(self._prefetch_block(j_next), i_next).start()

        return self.wb_ref.at[j % 2]

    def after_matmul(self):
        self._round((self.j + 1) % 2, self.i)


def _mm_wstream_kernel(a_ref, w_hbm, o_ref, wb_ref, stage_ref, sem, *, layer, n_chunks):
    stream = _WeightStream(w_hbm, wb_ref, stage_ref, sem, layer, o_ref.shape[1])
    w = stream.before_matmul(n_chunks)
    o_ref[...] = jnp.dot(a_ref[...], w[...], preferred_element_type=F32).astype(o_ref.dtype)
    stream.after_matmul()


def _mm_wstream_residual_kernel(a_ref, w_hbm, r_ref, o_ref, wb_ref, stage_ref, sem, *, layer, n_chunks):
    stream = _WeightStream(w_hbm, wb_ref, stage_ref, sem, layer, o_ref.shape[1])
    w = stream.before_matmul(n_chunks)
    acc = jnp.dot(a_ref[...], w[...], preferred_element_type=F32)
    o_ref[...] = (r_ref[...] + acc).astype(o_ref.dtype)
    stream.after_matmul()


def _weight_stream_scratch(k, tn, n_chunks):
    assert n_chunks % STAGE_SLOTS == 0
    return [pltpu.VMEM((2, k, tn), BF16),
            pltpu.VMEM((STAGE_SLOTS, k // n_chunks, tn), F32),
            pltpu.SemaphoreType.DMA((STAGE_SLOTS,))]


def matmul_f32_weights(a, w_stack, layer, out_dtype, tm, tn, residual=None, name="matmul_w"):
    m, k = a.shape
    n = w_stack.shape[-1]
    n_chunks = m // tm
    in_specs = [pl.BlockSpec((tm, k), lambda j, i: (i, 0)),
                pl.BlockSpec(memory_space=pl.ANY)]
    args = [a, w_stack]
    body = _mm_wstream_kernel
    if residual is not None:
        in_specs.append(pl.BlockSpec((tm, tn), lambda j, i: (i, j)))
        args.append(residual)
        body = _mm_wstream_residual_kernel
    return pl.pallas_call(
        functools.partial(body, layer=layer, n_chunks=n_chunks),
        out_shape=jax.ShapeDtypeStruct((m, n), out_dtype),
        grid=(n // tn, n_chunks),
        in_specs=in_specs,
        out_specs=pl.BlockSpec((tm, tn), lambda j, i: (i, j)),
        scratch_shapes=_weight_stream_scratch(k, tn, n_chunks),
        compiler_params=_cparams(("arbitrary", "arbitrary")),
        name=name,
    )(*args)


MOBA_TQ = MOBA_BLOCK


ALIBI_TERMS = 3


def _alibi_key_columns(slope, first_col, base):
    pos = lax.broadcasted_iota(jnp.int32, (SEQ, HEAD_DIM), 0) - SEQ // 2
    lane = lax.broadcasted_iota(jnp.int32, (SEQ, HEAD_DIM), 1)
    rest = (slope * HEAD_DIM ** 0.5) * pos.astype(F32)
    cols = base
    for term in range(ALIBI_TERMS):
        piece = rest.astype(BF16).astype(F32)
        rest = rest - piece
        cols = jnp.where(lane == first_col + term, piece, cols)
    return cols


def _alibi_query_columns(first_col, base):
    lane = lax.broadcasted_iota(jnp.int32, base.shape, 1)
    return jnp.where(lane < first_col, base, jnp.where(lane < first_col + ALIBI_TERMS, 1.0, base))


def _moba_kernel(slopes_ref, q_ref, k_ref, v_ref, z_ref, o_ref, qa_ref, ka_ref, va_ref):
    h = pl.program_id(1)
    nb, blk, hd = N_MOBA_BLOCKS, MOBA_BLOCK, HEAD_DIM

    k = k_ref[...]
    ka_ref[:, :hd] = k
    blk_shift = blk.bit_length() - 1
    key_blk = lax.broadcasted_iota(jnp.int32, (SEQ, hd), 0) >> blk_shift
    block_onehot = jnp.where(key_blk == lax.broadcasted_iota(jnp.int32, (SEQ, hd), 1), 1.0, 0.0)
    ka_ref[:, hd:] = _alibi_key_columns(slopes_ref[B_HEADS + h], nb, block_onehot).astype(BF16)
    va_ref[:, :hd] = v_ref[...]
    va_ref[:, hd:] = jnp.ones((SEQ, hd), BF16)

    km = jnp.concatenate(
        [jnp.mean(k[n * blk:(n + 1) * blk, :].astype(F32), axis=0, keepdims=True) for n in range(nb)], axis=0)
    q = q_ref[...]
    gate_t = lax.dot_general(km, q.astype(F32), NT_DIMS, precision=lax.Precision.HIGHEST,
                             preferred_element_type=F32)
    n_idx = lax.broadcasted_iota(jnp.int32, (nb, SEQ), 0)
    q_blk = lax.broadcasted_iota(jnp.int32, (nb, SEQ), 1) >> blk_shift
    past = n_idx < q_blk
    g = jnp.where(past, gate_t, NEG)
    rank = jnp.zeros((nb, SEQ), jnp.int32)
    for m in range(nb):
        row = g[m:m + 1, :]
        rank = rank + jnp.where(row > g, 1, jnp.where(row == g, jnp.where(n_idx > m, 1, 0), 0))
    mask_t = jnp.where(past, jnp.where(rank < MOBA_TOPK, 0.0, NEG),
                       jnp.where(n_idx == q_blk, 0.0, NEG))
    mask_rows = jnp.concatenate([mask_t, jnp.zeros((hd - nb, SEQ), F32)], axis=0).T
    qa_ref[:, :hd] = q
    qa_ref[:, hd:] = _alibi_query_columns(nb, mask_rows).astype(BF16)

    c1 = (hd ** -0.5) * LOG2E
    r_idx = lax.broadcasted_iota(jnp.int32, (MOBA_TQ, blk), 0)
    c_idx = lax.broadcasted_iota(jnp.int32, (MOBA_TQ, blk), 1)
    causal = r_idx >= c_idx

    def raw_scores(i):
        return lax.dot_general(qa_ref[i * MOBA_TQ:(i + 1) * MOBA_TQ, :], ka_ref[:(i + 1) * blk, :], NT_DIMS,
                               preferred_element_type=F32)

    s_next = raw_scores(0)
    for i in range(nb):
        rows = slice(i * MOBA_TQ, (i + 1) * MOBA_TQ)
        width = (i + 1) * blk
        s = s_next
        if i + 1 < nb:
            s_next = raw_scores(i + 1)
        s_own = jnp.where(causal, s[:, i * blk:], NEG)
        s = s_own if i == 0 else jnp.concatenate([s[:, :i * blk], s_own], axis=1)
        p = jnp.exp2((s - jnp.max(s, axis=-1, keepdims=True)) * c1)
        oa = jnp.dot(p.astype(BF16), va_ref[:width, :], preferred_element_type=F32)
        out = oa[:, :hd] / oa[:, hd:]
        o_ref[rows, :] = (out * jax.nn.silu(z_ref[rows, :].astype(F32))).astype(o_ref.dtype)


def moba_branch(proj, slopes):
    hpb = BW // HEAD_DIM
    seg = lambda c: pl.BlockSpec((SEQ, HEAD_DIM), lambda b, h: (b, c * hpb + h))
    return pl.pallas_call(
        _moba_kernel,
        out_shape=jax.ShapeDtypeStruct((ROWS, BW), BF16),
        grid=(BATCH, A_HEADS),
        in_specs=[pl.BlockSpec(memory_space=pltpu.SMEM), seg(COL_QA), seg(COL_KA), seg(COL_VA), seg(COL_ZA)],
        out_specs=pl.BlockSpec((SEQ, HEAD_DIM), lambda b, h: (b, h)),
        scratch_shapes=[pltpu.VMEM((SEQ, 2 * HEAD_DIM), BF16),
                        pltpu.VMEM((SEQ, 2 * HEAD_DIM), BF16),
                        pltpu.VMEM((SEQ, 2 * HEAD_DIM), BF16)],
        compiler_params=_cparams(("parallel", "parallel")),
        name="moba_branch",
    )(slopes, proj, proj, proj, proj)


DIFF_TQ = 256


def _diff_kernel(slopes_ref, lq1_ref, lk1_ref, lq2_ref, lk2_ref, subg_ref, q_ref, k_ref, v_ref, z_ref,
                 o_ref, qa_ref, ka_ref, *, lam_init):
    h = pl.program_id(1)
    tq, hd = DIFF_TQ, HEAD_DIM
    c1 = (hd ** -0.5) * LOG2E
    zeros = jnp.zeros((SEQ, hd), F32)
    q_aug = _alibi_query_columns(0, zeros).astype(BF16)
    k_aug = _alibi_key_columns(slopes_ref[h], 0, zeros).astype(BF16)
    for mp in range(2):
        qa_ref[:, 2 * mp * hd:(2 * mp + 1) * hd] = q_ref[:, mp * hd:(mp + 1) * hd]
        qa_ref[:, (2 * mp + 1) * hd:(2 * mp + 2) * hd] = q_aug
        ka_ref[:, 2 * mp * hd:(2 * mp + 1) * hd] = k_ref[:, mp * hd:(mp + 1) * hd]
        ka_ref[:, (2 * mp + 1) * hd:(2 * mp + 2) * hd] = k_aug
    lam = (jnp.exp(jnp.sum(lq1_ref[...] * lk1_ref[...], axis=-1, keepdims=True))
           - jnp.exp(jnp.sum(lq2_ref[...] * lk2_ref[...], axis=-1, keepdims=True)) + lam_init)
    r_idx = lax.broadcasted_iota(jnp.int32, (tq, tq), 0)
    c_idx = lax.broadcasted_iota(jnp.int32, (tq, tq), 1)
    causal = r_idx >= c_idx
    n_tiles = SEQ // tq

    def raw_scores(i, mp):
        cols = slice(2 * mp * hd, (2 * mp + 2) * hd)
        return lax.dot_general(qa_ref[i * tq:(i + 1) * tq, cols], ka_ref[:(i + 1) * tq, cols], NT_DIMS,
                               preferred_element_type=F32)

    def softmax_numerator(s, i):
        s_own = jnp.where(causal, s[:, i * tq:], NEG)
        s = s_own if i == 0 else jnp.concatenate([s[:, :i * tq], s_own], axis=1)
        p = jnp.exp2((s - jnp.max(s, axis=-1, keepdims=True)) * c1)
        return p.astype(BF16), jnp.sum(p, axis=-1, keepdims=True)

    s0_next = raw_scores(0, 0)
    for i in range(n_tiles):
        rows = slice(i * tq, (i + 1) * tq)
        s0, s1 = s0_next, raw_scores(i, 1)
        p0, l0 = softmax_numerator(s0, i)
        if i + 1 < n_tiles:
            s0_next = raw_scores(i + 1, 0)
        p1, l1 = softmax_numerator(s1, i)
        o = jnp.dot(jnp.concatenate([p0, p1], axis=0), v_ref[:(i + 1) * tq, :], preferred_element_type=F32)
        ob = o[:tq] / l0 - lam * (o[tq:] / l1)
        ms = jnp.mean(ob * ob, axis=-1, keepdims=True)
        ob = ob * lax.rsqrt(ms + RMS_EPS) * subg_ref[...]
        ob = ob * (1.0 - lam_init)
        o_ref[rows, :] = (ob * jax.nn.silu(z_ref[rows, :].astype(F32))).astype(o_ref.dtype)


def diff_branch(proj, slopes, lq1, lk1, lq2, lk2, subln_g, lam_init):
    hpb = BW // B_V_DIM
    vec = lambda a: a.reshape(1, -1).astype(F32)
    small = lambda n: pl.BlockSpec((1, n), lambda b, h: (0, 0))
    seg = lambda c: pl.BlockSpec((SEQ, B_V_DIM), lambda b, h: (b, c * hpb + h))
    return pl.pallas_call(
        functools.partial(_diff_kernel, lam_init=lam_init),
        out_shape=jax.ShapeDtypeStruct((ROWS, BW), BF16),
        grid=(BATCH, B_HEADS),
        in_specs=[pl.BlockSpec(memory_space=pltpu.SMEM),
                  small(HEAD_DIM), small(HEAD_DIM), small(HEAD_DIM), small(HEAD_DIM), small(B_V_DIM),
                  seg(COL_QB), seg(COL_KB), seg(COL_VB), seg(COL_ZB)],
        out_specs=pl.BlockSpec((SEQ, B_V_DIM), lambda b, h: (b, h)),
        scratch_shapes=[pltpu.VMEM((SEQ, 4 * HEAD_DIM), BF16), pltpu.VMEM((SEQ, 4 * HEAD_DIM), BF16)],
        compiler_params=_cparams(("parallel", "parallel")),
        name="diff_branch",
    )(slopes, vec(lq1), vec(lk1), vec(lq2), vec(lk2), vec(subln_g), proj, proj, proj, proj)


SGU_TILE_CHUNKS = 4


def _sgu_kernel(u_ref, v_ref, z_ref, lng_ref, lnb_ref, w_ref, bt_ref, o_ref):
    v = jax.nn.gelu(v_ref[...].astype(F32))
    mu = jnp.mean(v, axis=-1, keepdims=True)
    var = jnp.mean(jnp.square(v - mu), axis=-1, keepdims=True)
    vn = ((v - mu) * lax.rsqrt(var + LN_EPS) * lng_ref[...] + lnb_ref[...]).astype(BF16)
    r_idx = lax.broadcasted_iota(jnp.int32, (SGU_CHUNK, SGU_CHUNK), 0)
    c_idx = lax.broadcasted_iota(jnp.int32, (SGU_CHUNK, SGU_CHUNK), 1)
    causal = r_idx >= c_idx
    chunk_rows = [slice(c * SGU_CHUNK, (c + 1) * SGU_CHUNK) for c in range(SGU_TILE_CHUNKS)]
    for g in range(SGU_GROUPS):
        cols = slice(g * SGU_GROUP_CH, (g + 1) * SGU_GROUP_CH)
        w = jnp.where(causal, w_ref[g], 0.0).astype(BF16)
        vn_g = jnp.concatenate([vn[rows, cols] for rows in chunk_rows], axis=1)
        mixed_g = jnp.dot(w, vn_g, preferred_element_type=F32) + bt_ref[:, g:g + 1]
        for c, rows in enumerate(chunk_rows):
            mixed = mixed_g[:, c * SGU_GROUP_CH:(c + 1) * SGU_GROUP_CH]
            u = jax.nn.gelu(u_ref[rows, cols].astype(F32))
            o_ref[rows, cols] = (u * mixed * jax.nn.silu(z_ref[rows, cols].astype(F32))).astype(o_ref.dtype)


def sgu_branch(proj, ln_g, ln_b, w_s, b_s):
    tile = SGU_TILE_CHUNKS * SGU_CHUNK
    seg = lambda c: pl.BlockSpec((tile, BW), lambda t: (t, c))
    return pl.pallas_call(
        _sgu_kernel,
        out_shape=jax.ShapeDtypeStruct((ROWS, BW), BF16),
        grid=(ROWS // tile,),
        in_specs=[seg(COL_UC), seg(COL_VC), seg(COL_ZC),
                  pl.BlockSpec((1, BW), lambda t: (0, 0)),
                  pl.BlockSpec((1, BW), lambda t: (0, 0)),
                  pl.BlockSpec((SGU_GROUPS, SGU_CHUNK, SGU_CHUNK), lambda t: (0, 0, 0)),
                  pl.BlockSpec((SGU_CHUNK, SGU_GROUPS), lambda t: (0, 0))],
        out_specs=pl.BlockSpec((tile, BW), lambda t: (t, 0)),
        compiler_params=_cparams(("parallel",)),
        name="sgu_branch",
    )(proj, proj, proj, ln_g.reshape(1, BW), ln_b.reshape(1, BW), w_s, b_s.T)


MEM_TQ = 512


def _mem_kernel(q_ref, kv_ref, z_ref, o_ref):
    c1 = (M_HEAD_DIM ** -0.5) * LOG2E
    for h in range(M_HEADS):
        cols = slice(h * M_HEAD_DIM, (h + 1) * M_HEAD_DIM)
        k = kv_ref[:, cols]
        v = kv_ref[:, BW + h * M_HEAD_DIM:BW + (h + 1) * M_HEAD_DIM]
        for i in range(SEQ // MEM_TQ):
            rows = slice(i * MEM_TQ, (i + 1) * MEM_TQ)
            t = lax.dot_general(q_ref[rows, cols], k, NT_DIMS, preferred_element_type=F32) * c1
            p = jnp.exp2(t - jnp.max(t, axis=-1, keepdims=True))
            l = jnp.sum(p, axis=-1, keepdims=True)
            out = jnp.dot(p.astype(BF16), v, preferred_element_type=F32) / l
            o_ref[rows, cols] = (out * jax.nn.silu(z_ref[rows, cols].astype(F32))).astype(o_ref.dtype)


def mem_branch(proj, mem_kv):
    return pl.pallas_call(
        _mem_kernel,
        out_shape=jax.ShapeDtypeStruct((ROWS, BW), BF16),
        grid=(BATCH,),
        in_specs=[pl.BlockSpec((SEQ, BW), lambda b: (b, COL_QM)),
                  pl.BlockSpec((MEM_LEN, 2 * BW), lambda b: (b, 0)),
                  pl.BlockSpec((SEQ, BW), lambda b: (b, COL_ZM))],
        out_specs=pl.BlockSpec((SEQ, BW), lambda b: (b, 0)),
        compiler_params=_cparams(("parallel",)),
        name="mem_branch",
    )(proj, mem_kv, proj)


MERGE_TM = 512
MERGE_TN = 1024


def _merge_kernel(ya_ref, yb_ref, yc_ref, ym_ref, w_hbm, g0_ref, g1_ref, g2_ref, g3_ref, o_ref,
                  wb_ref, stage_ref, sem, *, layer, n_chunks):
    stream = _WeightStream(w_hbm, wb_ref, stage_ref, sem, layer, o_ref.shape[1])
    w = stream.before_matmul(n_chunks)
    ys = (ya_ref, yb_ref, yc_ref, ym_ref)
    gs = (g0_ref, g1_ref, g2_ref, g3_ref)
    merged = None
    for br in range(N_BRANCH):
        gate = jax.nn.sigmoid(gs[br][...].astype(F32))
        term = gate * jnp.dot(ys[br][...], w[br * BW:(br + 1) * BW, :], preferred_element_type=F32)
        merged = term if merged is None else merged + term
    o_ref[...] = merged.astype(o_ref.dtype)
    stream.after_matmul()


def gated_merge(ya, yb, yc, ym, w_branch, layer, proj):
    tm, tn = MERGE_TM, MERGE_TN
    n_chunks = ROWS // tm
    k = N_BRANCH * BW
    y_spec = pl.BlockSpec((tm, BW), lambda j, i: (i, 0))
    gate_spec = lambda br: pl.BlockSpec(
        (tm, tn), lambda j, i: (i, (COL_GATE * BW + br * D_MODEL) // tn + j))
    return pl.pallas_call(
        functools.partial(_merge_kernel, layer=layer, n_chunks=n_chunks),
        out_shape=jax.ShapeDtypeStruct((ROWS, D_MODEL), BF16),
        grid=(D_MODEL // tn, n_chunks),
        in_specs=[y_spec, y_spec, y_spec, y_spec,
                  pl.BlockSpec(memory_space=pl.ANY),
                  gate_spec(0), gate_spec(1), gate_spec(2), gate_spec(3)],
        out_specs=pl.BlockSpec((tm, tn), lambda j, i: (i, j)),
        scratch_shapes=_weight_stream_scratch(k, tn, n_chunks),
        compiler_params=_cparams(("arbitrary", "arbitrary")),
        name="gated_merge",
    )(ya, yb, yc, ym, w_branch.reshape(DEPTH, k, D_MODEL), proj, proj, proj, proj)


def kernel(x, mem, norm_g, w_in, mem_norm_g, w_mem_kv, diff_lam_q1, diff_lam_k1, diff_lam_q2, diff_lam_k2,
           diff_subln_g, sgu_ln_g, sgu_ln_b, sgu_w, sgu_b, w_branch, w_out, final_g):
    xf = x.reshape(ROWS, D_MODEL)
    memf = mem.reshape(BATCH * MEM_LEN, D_MODEL)
    slopes = jnp.exp2(-8.0 * jnp.arange(1, N_ALIBI_HEADS + 1, dtype=F32) / N_ALIBI_HEADS)
    for l in range(DEPTH):
        lam_init = 0.8 - 0.6 * math.exp(-0.3 * l)
        h = rmsnorm(xf, norm_g[l], BF16)
        proj = matmul_f32_weights(h, w_in, l, BF16, tm=1024, tn=1024, name="in_proj")
        hm = rmsnorm(memf, mem_norm_g[l], BF16)
        mem_kv = matmul_f32_weights(hm, w_mem_kv, l, BF16, tm=256, tn=1024, name="mem_kv_proj")
        ya = moba_branch(proj, slopes)
        yb = diff_branch(proj, slopes, diff_lam_q1[l], diff_lam_k1[l], diff_lam_q2[l], diff_lam_k2[l],
                         diff_subln_g[l], lam_init)
        yc = sgu_branch(proj, sgu_ln_g[l], sgu_ln_b[l], sgu_w[l], sgu_b[l])
        ym = mem_branch(proj, mem_kv)
        merged = gated_merge(ya, yb, yc, ym, w_branch, l, proj)
        xf = matmul_f32_weights(merged, w_out, l, F32, tm=512, tn=1024, residual=xf, name="out_proj")
    out = rmsnorm(xf, final_g, F32)
    return out.reshape(BATCH, SEQ, D_MODEL)
```

```python
import functools
import math

import jax
import jax.numpy as jnp
from jax import lax
from jax.experimental import pallas as pl
from jax.experimental.pallas import tpu as pltpu

D_MODEL = 4096
BATCH = 4
SEQ = 2048
DEPTH = 2
MEM_LEN = 256
HEAD_DIM = 128
BW = D_MODEL // 4
N_BRANCH = 4
A_HEADS = BW // HEAD_DIM
MOBA_BLOCK = 256
MOBA_TOPK = 3
B_HEADS = BW // (2 * HEAD_DIM)
B_V_DIM = 2 * HEAD_DIM
SGU_CHUNK = 128
SGU_GROUP_CH = 128
SGU_GROUPS = BW // SGU_GROUP_CH
M_HEADS = 4
M_HEAD_DIM = BW // M_HEADS
N_ALIBI_HEADS = A_HEADS + B_HEADS
RMS_EPS = 1e-6
LN_EPS = 1e-5
NEG = -1e30
N_MOBA_BLOCKS = SEQ // MOBA_BLOCK

(COL_QA, COL_KA, COL_VA, COL_ZA, COL_QB, COL_KB, COL_VB, COL_ZB,
 COL_UC, COL_VC, COL_ZC, COL_QM, COL_ZM, COL_GATE) = range(14)

ROWS = BATCH * SEQ
F32 = jnp.float32
BF16 = jnp.bfloat16
LOG2E = math.log2(math.e)

VMEM_LIMIT_BYTES = 58 * 1024 * 1024

NT_DIMS = (((1,), (1,)), ((), ()))


def _cparams(semantics):
    return pltpu.CompilerParams(dimension_semantics=semantics, vmem_limit_bytes=VMEM_LIMIT_BYTES)


def _rmsnorm_kernel(x_ref, g_ref, o_ref):
    x = x_ref[...].astype(F32)
    ms = jnp.mean(x * x, axis=-1, keepdims=True)
    o_ref[...] = (x * lax.rsqrt(ms + RMS_EPS) * g_ref[...]).astype(o_ref.dtype)


def rmsnorm(x, g, out_dtype, tile_rows=512):
    rows, d = x.shape
    return pl.pallas_call(
        _rmsnorm_kernel,
        out_shape=jax.ShapeDtypeStruct((rows, d), out_dtype),
        grid=(rows // tile_rows,),
        in_specs=[pl.BlockSpec((tile_rows, d), lambda i: (i, 0)),
                  pl.BlockSpec((1, d), lambda i: (0, 0))],
        out_specs=pl.BlockSpec((tile_rows, d), lambda i: (i, 0)),
        compiler_params=_cparams(("parallel",)),
        name="rmsnorm",
    )(x, g.reshape(1, d).astype(F32))


STAGE_SLOTS = 4


class _WeightStream:
    def __init__(self, w_hbm, wb_ref, stage_ref, sem, layer, tn):
        self.w_hbm, self.wb_ref, self.stage_ref, self.sem = w_hbm, wb_ref, stage_ref, sem
        self.layer, self.tn = layer, tn
        self.j, self.i = pl.program_id(0), pl.program_id(1)
        self.nj, self.ni = pl.num_programs(0), pl.num_programs(1)
        self.chunk_rows = stage_ref.shape[1]

    def _rows(self, chunk):
        return pl.ds(pl.multiple_of(chunk * self.chunk_rows, self.chunk_rows), self.chunk_rows)

    def _copy(self, block, chunk):
        cols = pl.ds(pl.multiple_of(block * self.tn, self.tn), self.tn)
        src = self.w_hbm.at[self.layer, self._rows(chunk), cols]
        slot = chunk % STAGE_SLOTS
        return pltpu.make_async_copy(src, self.stage_ref.at[slot], self.sem.at[slot])

    def _round(self, wb_slot, chunk):
        self.wb_ref[wb_slot, self._rows(chunk), :] = self.stage_ref[chunk % STAGE_SLOTS].astype(BF16)

    def _prefetch_block(self, j):
        return jnp.minimum(j + 1, self.nj - 1)

    def before_matmul(self, n_chunks):
        j, i = self.j, self.i
        step = j * self.ni + i

        @pl.when(step == 0)
        def _():
            depth = STAGE_SLOTS - 1
            for c in range(depth):
                self._copy(0, c).start()
            for c in range(n_chunks):
                if c + depth < n_chunks:
                    self._copy(0, c + depth).start()
                self._copy(0, c).wait()
                self._round(0, c)
            self._copy(self._prefetch_block(0), 0).start()

        self._copy(self._prefetch_block(j), i).wait()

        @pl.when(step + 1 < self.nj * self.ni)
        def _():
            wrap = i + 1 == self.ni
            i_next = jnp.where(wrap, 0, i + 1)
            j_next = jnp.where(wrap, j + 1, j)
            self._copy(self._prefetch_block(j_next), i_next).start()

        self._round((j + 1) % 2, i)
        return self.wb_ref.at[j % 2]


def _mm_wstream_kernel(a_ref, w_hbm, o_ref, wb_ref, stage_ref, sem, *, layer, n_chunks):
    stream = _WeightStream(w_hbm, wb_ref, stage_ref, sem, layer, o_ref.shape[1])
    w = stream.before_matmul(n_chunks)
    o_ref[...] = jnp.dot(a_ref[...], w[...], preferred_element_type=F32).astype(o_ref.dtype)


def _mm_wstream_residual_kernel(a_ref, w_hbm, r_ref, o_ref, wb_ref, stage_ref, sem, *, layer, n_chunks):
    stream = _WeightStream(w_hbm, wb_ref, stage_ref, sem, layer, o_ref.shape[1])
    w = stream.before_matmul(n_chunks)
    acc = jnp.dot(a_ref[...], w[...], preferred_element_type=F32)
    o_ref[...] = (r_ref[...] + acc).astype(o_ref.dtype)


def _weight_stream_scratch(k, tn, n_chunks):
    assert n_chunks % STAGE_SLOTS == 0
    return [pltpu.VMEM((2, k, tn), BF16),
            pltpu.VMEM((STAGE_SLOTS, k // n_chunks, tn), F32),
            pltpu.SemaphoreType.DMA((STAGE_SLOTS,))]


def matmul_f32_weights(a, w_stack, layer, out_dtype, tm, tn, residual=None, name="matmul_w"):
    m, k = a.shape
    n = w_stack.shape[-1]
    n_chunks = m // tm
    in_specs = [pl.BlockSpec((tm, k), lambda j, i: (i, 0)),
                pl.BlockSpec(memory_space=pl.ANY)]
    args = [a, w_stack]
    body = _mm_wstream_kernel
    if residual is not None:
        in_specs.append(pl.BlockSpec((tm, tn), lambda j, i: (i, j)))
        args.append(residual)
        body = _mm_wstream_residual_kernel
    return pl.pallas_call(
        functools.partial(body, layer=layer, n_chunks=n_chunks),
        out_shape=jax.ShapeDtypeStruct((m, n), out_dtype),
        grid=(n // tn, n_chunks),
        in_specs=in_specs,
        out_specs=pl.BlockSpec((tm, tn), lambda j, i: (i, j)),
        scratch_shapes=_weight_stream_scratch(k, tn, n_chunks),
        compiler_params=_cparams(("arbitrary", "arbitrary")),
        name=name,
    )(*args)


MOBA_TQ = MOBA_BLOCK


ALIBI_TERMS = 3


def _alibi_key_columns(slope, first_col, base):
    pos = lax.broadcasted_iota(jnp.int32, (SEQ, HEAD_DIM), 0) - SEQ // 2
    lane = lax.broadcasted_iota(jnp.int32, (SEQ, HEAD_DIM), 1)
    rest = (slope * HEAD_DIM ** 0.5) * pos.astype(F32)
    cols = base
    for term in range(ALIBI_TERMS):
        piece = rest.astype(BF16).astype(F32)
        rest = rest - piece
        cols = jnp.where(lane == first_col + term, piece, cols)
    return cols


def _alibi_query_columns(first_col, base):
    lane = lax.broadcasted_iota(jnp.int32, base.shape, 1)
    return jnp.where(lane < first_col, base, jnp.where(lane < first_col + ALIBI_TERMS, 1.0, base))


def _moba_kernel(slopes_ref, q_ref, k_ref, v_ref, z_ref, o_ref, qa_ref, ka_ref, va_ref):
    h = pl.program_id(1)
    nb, blk, hd = N_MOBA_BLOCKS, MOBA_BLOCK, HEAD_DIM

    k = k_ref[...]
    ka_ref[:, :hd] = k
    blk_shift = blk.bit_length() - 1
    key_blk = lax.broadcasted_iota(jnp.int32, (SEQ, hd), 0) >> blk_shift
    block_onehot = jnp.where(key_blk == lax.broadcasted_iota(jnp.int32, (SEQ, hd), 1), 1.0, 0.0)
    ka_ref[:, hd:] = _alibi_key_columns(slopes_ref[B_HEADS + h], nb, block_onehot).astype(BF16)
    va_ref[:, :hd] = v_ref[...]
    va_ref[:, hd:] = jnp.ones((SEQ, hd), BF16)

    km = jnp.concatenate(
        [jnp.mean(k[n * blk:(n + 1) * blk, :].astype(F32), axis=0, keepdims=True) for n in range(nb)], axis=0)
    q = q_ref[...]
    gate_t = lax.dot_general(km, q.astype(F32), NT_DIMS, precision=lax.Precision.HIGHEST,
                             preferred_element_type=F32)
    n_idx = lax.broadcasted_iota(jnp.int32, (nb, SEQ), 0)
    q_blk = lax.broadcasted_iota(jnp.int32, (nb, SEQ), 1) >> blk_shift
    past = n_idx < q_blk
    g = jnp.where(past, gate_t, NEG)
    rank = jnp.zeros((nb, SEQ), jnp.int32)
    for m in range(nb):
        row = g[m:m + 1, :]
        rank = rank + jnp.where(row > g, 1, jnp.where(row == g, jnp.where(n_idx > m, 1, 0), 0))
    mask_t = jnp.where(past, jnp.where(rank < MOBA_TOPK, 0.0, NEG),
                       jnp.where(n_idx == q_blk, 0.0, NEG))
    mask_rows = jnp.concatenate([mask_t, jnp.zeros((hd - nb, SEQ), F32)], axis=0).T
    qa_ref[:, :hd] = q
    qa_ref[:, hd:] = _alibi_query_columns(nb, mask_rows).astype(BF16)

    c1 = (hd ** -0.5) * LOG2E
    r_idx = lax.broadcasted_iota(jnp.int32, (MOBA_TQ, blk), 0)
    c_idx = lax.broadcasted_iota(jnp.int32, (MOBA_TQ, blk), 1)
    causal = r_idx >= c_idx

    def raw_scores(i):
        return lax.dot_general(qa_ref[i * MOBA_TQ:(i + 1) * MOBA_TQ, :], ka_ref[:(i + 1) * blk, :], NT_DIMS,
                               preferred_element_type=F32)

    s_next = raw_scores(0)
    for i in range(nb):
        rows = slice(i * MOBA_TQ, (i + 1) * MOBA_TQ)
        width = (i + 1) * blk
        s = s_next
        if i + 1 < nb:
            s_next = raw_scores(i + 1)
        s_own = jnp.where(causal, s[:, i * blk:], NEG)
        s = s_own if i == 0 else jnp.concatenate([s[:, :i * blk], s_own], axis=1)
        p = jnp.exp2((s - jnp.max(s, axis=-1, keepdims=True)) * c1)
        oa = jnp.dot(p.astype(BF16), va_ref[:width, :], preferred_element_type=F32)
        out = oa[:, :hd] / oa[:, hd:]
        o_ref[rows, :] = (out * jax.nn.silu(z_ref[rows, :].astype(F32))).astype(o_ref.dtype)


def moba_branch(proj, slopes):
    hpb = BW // HEAD_DIM
    seg = lambda c: pl.BlockSpec((SEQ, HEAD_DIM), lambda b, h: (b, c * hpb + h))
    return pl.pallas_call(
        _moba_kernel,
        out_shape=jax.ShapeDtypeStruct((ROWS, BW), BF16),
        grid=(BATCH, A_HEADS),
        in_specs=[pl.BlockSpec(memory_space=pltpu.SMEM), seg(COL_QA), seg(COL_KA), seg(COL_VA), seg(COL_ZA)],
        out_specs=pl.BlockSpec((SEQ, HEAD_DIM), lambda b, h: (b, h)),
        scratch_shapes=[pltpu.VMEM((SEQ, 2 * HEAD_DIM), BF16),
                        pltpu.VMEM((SEQ, 2 * HEAD_DIM), BF16),
                        pltpu.VMEM((SEQ, 2 * HEAD_DIM), BF16)],
        compiler_params=_cparams(("parallel", "parallel")),
        name="moba_branch",
    )(slopes, proj, proj, proj, proj)


DIFF_TQ = 256


def _diff_kernel(slopes_ref, lq1_ref, lk1_ref, lq2_ref, lk2_ref, subg_ref, q_ref, k_ref, v_ref, z_ref,
                 o_ref, qa_ref, ka_ref, *, lam_init):
    h = pl.program_id(1)
    tq, hd = DIFF_TQ, HEAD_DIM
    c1 = (hd ** -0.5) * LOG2E
    zeros = jnp.zeros((SEQ, hd), F32)
    q_aug = _alibi_query_columns(0, zeros).astype(BF16)
    k_aug = _alibi_key_columns(slopes_ref[h], 0, zeros).astype(BF16)
    for mp in range(2):
        qa_ref[:, 2 * mp * hd:(2 * mp + 1) * hd] = q_ref[:, mp * hd:(mp + 1) * hd]
        qa_ref[:, (2 * mp + 1) * hd:(2 * mp + 2) * hd] = q_aug
        ka_ref[:, 2 * mp * hd:(2 * mp + 1) * hd] = k_ref[:, mp * hd:(mp + 1) * hd]
        ka_ref[:, (2 * mp + 1) * hd:(2 * mp + 2) * hd] = k_aug
    lam = (jnp.exp(jnp.sum(lq1_ref[...] * lk1_ref[...], axis=-1, keepdims=True))
           - jnp.exp(jnp.sum(lq2_ref[...] * lk2_ref[...], axis=-1, keepdims=True)) + lam_init)
    r_idx = lax.broadcasted_iota(jnp.int32, (tq, tq), 0)
    c_idx = lax.broadcasted_iota(jnp.int32, (tq, tq), 1)
    causal = r_idx >= c_idx
    n_tiles = SEQ // tq

    def raw_scores(i, mp):
        cols = slice(2 * mp * hd, (2 * mp + 2) * hd)
        return lax.dot_general(qa_ref[i * tq:(i + 1) * tq, cols], ka_ref[:(i + 1) * tq, cols], NT_DIMS,
                               preferred_element_type=F32)

    def softmax_numerator(s, i):
        s_own = jnp.where(causal, s[:, i * tq:], NEG)
        s = s_own if i == 0 else jnp.concatenate([s[:, :i * tq], s_own], axis=1)
        p = jnp.exp2((s - jnp.max(s, axis=-1, keepdims=True)) * c1)
        return p.astype(BF16), jnp.sum(p, axis=-1, keepdims=True)

    s0_next = raw_scores(0, 0)
    for i in range(n_tiles):
        rows = slice(i * tq, (i + 1) * tq)
        s0, s1 = s0_next, raw_scores(i, 1)
        p0, l0 = softmax_numerator(s0, i)
        if i + 1 < n_tiles:
            s0_next = raw_scores(i + 1, 0)
        p1, l1 = softmax_numerator(s1, i)
        o = jnp.dot(jnp.concatenate([p0, p1], axis=0), v_ref[:(i + 1) * tq, :], preferred_element_type=F32)
        ob = o[:tq] / l0 - lam * (o[tq:] / l1)
        ms = jnp.mean(ob * ob, axis=-1, keepdims=True)
        ob = ob * lax.rsqrt(ms + RMS_EPS) * subg_ref[...]
        ob = ob * (1.0 - lam_init)
        o_ref[rows, :] = (ob * jax.nn.silu(z_ref[rows, :].astype(F32))).astype(o_ref.dtype)


def diff_branch(proj, slopes, lq1, lk1, lq2, lk2, subln_g, lam_init):
    hpb = BW // B_V_DIM
    vec = lambda a: a.reshape(1, -1).astype(F32)
    small = lambda n: pl.BlockSpec((1, n), lambda b, h: (0, 0))
    seg = lambda c: pl.BlockSpec((SEQ, B_V_DIM), lambda b, h: (b, c * hpb + h))
    return pl.pallas_call(
        functools.partial(_diff_kernel, lam_init=lam_init),
        out_shape=jax.ShapeDtypeStruct((ROWS, BW), BF16),
        grid=(BATCH, B_HEADS),
        in_specs=[pl.BlockSpec(memory_space=pltpu.SMEM),
                  small(HEAD_DIM), small(HEAD_DIM), small(HEAD_DIM), small(HEAD_DIM), small(B_V_DIM),
                  seg(COL_QB), seg(COL_KB), seg(COL_VB), seg(COL_ZB)],
        out_specs=pl.BlockSpec((SEQ, B_V_DIM), lambda b, h: (b, h)),
        scratch_shapes=[pltpu.VMEM((SEQ, 4 * HEAD_DIM), BF16), pltpu.VMEM((SEQ, 4 * HEAD_DIM), BF16)],
        compiler_params=_cparams(("parallel", "parallel")),
        name="diff_branch",
    )(slopes, vec(lq1), vec(lk1), vec(lq2), vec(lk2), vec(subln_g), proj, proj, proj, proj)


SGU_TILE_CHUNKS = 8


def _sgu_kernel(u_ref, v_ref, z_ref, lng_ref, lnb_ref, w_ref, bt_ref, o_ref):
    v = jax.nn.gelu(v_ref[...].astype(F32))
    mu = jnp.mean(v, axis=-1, keepdims=True)
    var = jnp.mean(jnp.square(v - mu), axis=-1, keepdims=True)
    vn = ((v - mu) * lax.rsqrt(var + LN_EPS) * lng_ref[...] + lnb_ref[...]).astype(BF16)
    r_idx = lax.broadcasted_iota(jnp.int32, (SGU_CHUNK, SGU_CHUNK), 0)
    c_idx = lax.broadcasted_iota(jnp.int32, (SGU_CHUNK, SGU_CHUNK), 1)
    causal = r_idx >= c_idx
    chunk_rows = [slice(c * SGU_CHUNK, (c + 1) * SGU_CHUNK) for c in range(SGU_TILE_CHUNKS)]
    for g in range(SGU_GROUPS):
        cols = slice(g * SGU_GROUP_CH, (g + 1) * SGU_GROUP_CH)
        w = jnp.where(causal, w_ref[g], 0.0).astype(BF16)
        vn_g = jnp.concatenate([vn[rows, cols] for rows in chunk_rows], axis=1)
        mixed_g = jnp.dot(w, vn_g, preferred_element_type=F32) + bt_ref[:, g:g + 1]
        for c, rows in enumerate(chunk_rows):
            mixed = mixed_g[:, c * SGU_GROUP_CH:(c + 1) * SGU_GROUP_CH]
            u = jax.nn.gelu(u_ref[rows, cols].astype(F32))
            o_ref[rows, cols] = (u * mixed * jax.nn.silu(z_ref[rows, cols].astype(F32))).astype(o_ref.dtype)


def sgu_branch(proj, ln_g, ln_b, w_s, b_s):
    tile = SGU_TILE_CHUNKS * SGU_CHUNK
    seg = lambda c: pl.BlockSpec((tile, BW), lambda t: (t, c))
    return pl.pallas_call(
        _sgu_kernel,
        out_shape=jax.ShapeDtypeStruct((ROWS, BW), BF16),
        grid=(ROWS // tile,),
        in_specs=[seg(COL_UC), seg(COL_VC), seg(COL_ZC),
                  pl.BlockSpec((1, BW), lambda t: (0, 0)),
                  pl.BlockSpec((1, BW), lambda t: (0, 0)),
                  pl.BlockSpec((SGU_GROUPS, SGU_CHUNK, SGU_CHUNK), lambda t: (0, 0, 0)),
                  pl.BlockSpec((SGU_CHUNK, SGU_GROUPS), lambda t: (0, 0))],
        out_specs=pl.BlockSpec((tile, BW), lambda t: (t, 0)),
        compiler_params=_cparams(("parallel",)),
        name="sgu_branch",
    )(proj, proj, proj, ln_g.reshape(1, BW), ln_b.reshape(1, BW), w_s, b_s.T)


MEM_TQ = 1024


def _mem_kernel(q_ref, kv_ref, z_ref, o_ref):
    c1 = (M_HEAD_DIM ** -0.5) * LOG2E
    for h in range(M_HEADS):
        cols = slice(h * M_HEAD_DIM, (h + 1) * M_HEAD_DIM)
        k = kv_ref[:, cols]
        v = kv_ref[:, BW + h * M_HEAD_DIM:BW + (h + 1) * M_HEAD_DIM]
        for i in range(SEQ // MEM_TQ):
            rows = slice(i * MEM_TQ, (i + 1) * MEM_TQ)
            t = lax.dot_general(q_ref[rows, cols], k, NT_DIMS, preferred_element_type=F32) * c1
            p = jnp.exp2(t - jnp.max(t, axis=-1, keepdims=True))
            l = jnp.sum(p, axis=-1, keepdims=True)
            out = jnp.dot(p.astype(BF16), v, preferred_element_type=F32) / l
            o_ref[rows, cols] = (out * jax.nn.silu(z_ref[rows, cols].astype(F32))).astype(o_ref.dtype)


def mem_branch(proj, mem_kv):
    return pl.pallas_call(
        _mem_kernel,
        out_shape=jax.ShapeDtypeStruct((ROWS, BW), BF16),
        grid=(BATCH,),
        in_specs=[pl.BlockSpec((SEQ, BW), lambda b: (b, COL_QM)),
                  pl.BlockSpec((MEM_LEN, 2 * BW), lambda b: (b, 0)),
                  pl.BlockSpec((SEQ, BW), lambda b: (b, COL_ZM))],
        out_specs=pl.BlockSpec((SEQ, BW), lambda b: (b, 0)),
        compiler_params=_cparams(("parallel",)),
        name="mem_branch",
    )(proj, mem_kv, proj)


MERGE_TM = 512
MERGE_TN = 1024


def _merge_kernel(ya_ref, yb_ref, yc_ref, ym_ref, w_hbm, g0_ref, g1_ref, g2_ref, g3_ref, o_ref,
                  wb_ref, stage_ref, sem, *, layer, n_chunks):
    stream = _WeightStream(w_hbm, wb_ref, stage_ref, sem, layer, o_ref.shape[1])
    w = stream.before_matmul(n_chunks)
    ys = (ya_ref, yb_ref, yc_ref, ym_ref)
    gs = (g0_ref, g1_ref, g2_ref, g3_ref)
    merged = None
    for br in range(N_BRANCH):
        gate = jax.nn.sigmoid(gs[br][...].astype(F32))
        term = gate * jnp.dot(ys[br][...], w[br * BW:(br + 1) * BW, :], preferred_element_type=F32)
        merged = term if merged is None else merged + term
    o_ref[...] = merged.astype(o_ref.dtype)


def gated_merge(ya, yb, yc, ym, w_branch, layer, proj):
    tm, tn = MERGE_TM, MERGE_TN
    n_chunks = ROWS // tm
    k = N_BRANCH * BW
    y_spec = pl.BlockSpec((tm, BW), lambda j, i: (i, 0))
    gate_spec = lambda br: pl.BlockSpec(
        (tm, tn), lambda j, i: (i, (COL_GATE * BW + br * D_MODEL) // tn + j))
    return pl.pallas_call(
        functools.partial(_merge_kernel, layer=layer, n_chunks=n_chunks),
        out_shape=jax.ShapeDtypeStruct((ROWS, D_MODEL), BF16),
        grid=(D_MODEL // tn, n_chunks),
        in_specs=[y_spec, y_spec, y_spec, y_spec,
                  pl.BlockSpec(memory_space=pl.ANY),
                  gate_spec(0), gate_spec(1), gate_spec(2), gate_spec(3)],
        out_specs=pl.BlockSpec((tm, tn), lambda j, i: (i, j)),
        scratch_shapes=_weight_stream_scratch(k, tn, n_chunks),
        compiler_params=_cparams(("arbitrary", "arbitrary")),
        name="gated_merge",
    )(ya, yb, yc, ym, w_branch.reshape(DEPTH, k, D_MODEL), proj, proj, proj, proj)


def kernel(x, mem, norm_g, w_in, mem_norm_g, w_mem_kv, diff_lam_q1, diff_lam_k1, diff_lam_q2, diff_lam_k2,
           diff_subln_g, sgu_ln_g, sgu_ln_b, sgu_w, sgu_b, w_branch, w_out, final_g):
    xf = x.reshape(ROWS, D_MODEL)
    memf = mem.reshape(BATCH * MEM_LEN, D_MODEL)
    slopes = jnp.exp2(-8.0 * jnp.arange(1, N_ALIBI_HEADS + 1, dtype=F32) / N_ALIBI_HEADS)
    for l in range(DEPTH):
        lam_init = 0.8 - 0.6 * math.exp(-0.3 * l)
        h = rmsnorm(xf, norm_g[l], BF16)
        proj = matmul_f32_weights(h, w_in, l, BF16, tm=1024, tn=1024, name="in_proj")
        hm = rmsnorm(memf, mem_norm_g[l], BF16)
        mem_kv = matmul_f32_weights(hm, w_mem_kv, l, BF16, tm=256, tn=1024, name="mem_kv_proj")
        ya = moba_branch(proj, slopes)
        yb = diff_branch(proj, slopes, diff_lam_q1[l], diff_lam_k1[l], diff_lam_q2[l], diff_lam_k2[l],
                         diff_subln_g[l], lam_init)
        yc = sgu_branch(proj, sgu_ln_g[l], sgu_ln_b[l], sgu_w[l], sgu_b[l])
        ym = mem_branch(proj, mem_kv)
        merged = gated_merge(ya, yb, yc, ym, w_branch, l, proj)
        xf = matmul_f32_weights(merged, w_out, l, F32, tm=512, tn=1024, residual=xf, name="out_proj")
    out = rmsnorm(xf, final_g, F32)
    return out.reshape(BATCH, SEQ, D_MODEL)
```

```python
import functools
import math

import jax
import jax.numpy as jnp
from jax import lax
from jax.experimental import pallas as pl
from jax.experimental.pallas import tpu as pltpu

D_MODEL = 4096
BATCH = 4
SEQ = 2048
DEPTH = 2
MEM_LEN = 256
HEAD_DIM = 128
BW = D_MODEL // 4
N_BRANCH = 4
A_HEADS = BW // HEAD_DIM
MOBA_BLOCK = 256
MOBA_TOPK = 3
B_HEADS = BW // (2 * HEAD_DIM)
B_V_DIM = 2 * HEAD_DIM
SGU_CHUNK = 128
SGU_GROUP_CH = 128
SGU_GROUPS = BW // SGU_GROUP_CH
M_HEADS = 4
M_HEAD_DIM = BW // M_HEADS
N_ALIBI_HEADS = A_HEADS + B_HEADS
RMS_EPS = 1e-6
LN_EPS = 1e-5
NEG = -1e30
N_MOBA_BLOCKS = SEQ // MOBA_BLOCK

(COL_QA, COL_KA, COL_VA, COL_ZA, COL_QB, COL_KB, COL_VB, COL_ZB,
 COL_UC, COL_VC, COL_ZC, COL_QM, COL_ZM, COL_GATE) = range(14)

ROWS = BATCH * SEQ
F32 = jnp.float32
BF16 = jnp.bfloat16
LOG2E = math.log2(math.e)

VMEM_LIMIT_BYTES = 58 * 1024 * 1024

NT_DIMS = (((1,), (1,)), ((), ()))


def _cparams(semantics):
    return pltpu.CompilerParams(dimension_semantics=semantics, vmem_limit_bytes=VMEM_LIMIT_BYTES)


def _rmsnorm_kernel(x_ref, g_ref, o_ref):
    x = x_ref[...].astype(F32)
    ms = jnp.mean(x * x, axis=-1, keepdims=True)
    o_ref[...] = (x * lax.rsqrt(ms + RMS_EPS) * g_ref[...]).astype(o_ref.dtype)


def rmsnorm(x, g, out_dtype, tile_rows=512):
    rows, d = x.shape
    return pl.pallas_call(
        _rmsnorm_kernel,
        out_shape=jax.ShapeDtypeStruct((rows, d), out_dtype),
        grid=(rows // tile_rows,),
        in_specs=[pl.BlockSpec((tile_rows, d), lambda i: (i, 0)),
                  pl.BlockSpec((1, d), lambda i: (0, 0))],
        out_specs=pl.BlockSpec((tile_rows, d), lambda i: (i, 0)),
        compiler_params=_cparams(("parallel",)),
        name="rmsnorm",
    )(x, g.reshape(1, d).astype(F32))


STAGE_SLOTS = 4


class _WeightStream:
    def __init__(self, w_hbm, wb_ref, stage_ref, sem, layer, tn):
        self.w_hbm, self.wb_ref, self.stage_ref, self.sem = w_hbm, wb_ref, stage_ref, sem
        self.layer, self.tn = layer, tn
        self.j, self.i = pl.program_id(0), pl.program_id(1)
        self.nj, self.ni = pl.num_programs(0), pl.num_programs(1)
        self.chunk_rows = stage_ref.shape[1]

    def _rows(self, chunk):
        return pl.ds(pl.multiple_of(chunk * self.chunk_rows, self.chunk_rows), self.chunk_rows)

    def _copy(self, block, chunk):
        cols = pl.ds(pl.multiple_of(block * self.tn, self.tn), self.tn)
        src = self.w_hbm.at[self.layer, self._rows(chunk), cols]
        slot = chunk % STAGE_SLOTS
        return pltpu.make_async_copy(src, self.stage_ref.at[slot], self.sem.at[slot])

    def _round(self, wb_slot, chunk):
        self.wb_ref[wb_slot, self._rows(chunk), :] = self.stage_ref[chunk % STAGE_SLOTS].astype(BF16)

    def _prefetch_block(self, j):
        return jnp.minimum(j + 1, self.nj - 1)

    def before_matmul(self, n_chunks):
        j, i = self.j, self.i
        step = j * self.ni + i

        @pl.when(step == 0)
        def _():
            depth = STAGE_SLOTS - 1
            for c in range(depth):
                self._copy(0, c).start()
            for c in range(n_chunks):
                if c + depth < n_chunks:
                    self._copy(0, c + depth).start()
                self._copy(0, c).wait()
                self._round(0, c)
            self._copy(self._prefetch_block(0), 0).start()

        self._copy(self._prefetch_block(j), i).wait()

        @pl.when(step + 1 < self.nj * self.ni)
        def _():
            wrap = i + 1 == self.ni
            i_next = jnp.where(wrap, 0, i + 1)
            j_next = jnp.where(wrap, j + 1, j)
            self._copy(self._prefetch_block(j_next), i_next).start()

        self._round((j + 1) % 2, i)
        return self.wb_ref.at[j % 2]


def _mm_wstream_kernel(a_ref, w_hbm, o_ref, wb_ref, stage_ref, sem, *, layer, n_chunks):
    stream = _WeightStream(w_hbm, wb_ref, stage_ref, sem, layer, o_ref.shape[1])
    w = stream.before_matmul(n_chunks)
    o_ref[...] = jnp.dot(a_ref[...], w[...], preferred_element_type=F32).astype(o_ref.dtype)


def _mm_wstream_residual_kernel(a_ref, w_hbm, r_ref, o_ref, wb_ref, stage_ref, sem, *, layer, n_chunks):
    stream = _WeightStream(w_hbm, wb_ref, stage_ref, sem, layer, o_ref.shape[1])
    w = stream.before_matmul(n_chunks)
    acc = jnp.dot(a_ref[...], w[...], preferred_element_type=F32)
    o_ref[...] = (r_ref[...] + acc).astype(o_ref.dtype)


def _weight_stream_scratch(k, tn, n_chunks):
    assert n_chunks % STAGE_SLOTS == 0
    return [pltpu.VMEM((2, k, tn), BF16),
            pltpu.VMEM((STAGE_SLOTS, k // n_chunks, tn), F32),
            pltpu.SemaphoreType.DMA((STAGE_SLOTS,))]


def matmul_f32_weights(a, w_stack, layer, out_dtype, tm, tn, residual=None, name="matmul_w"):
    m, k = a.shape
    n = w_stack.shape[-1]
    n_chunks = m // tm
    in_specs = [pl.BlockSpec((tm, k), lambda j, i: (i, 0)),
                pl.BlockSpec(memory_space=pl.ANY)]
    args = [a, w_stack]
    body = _mm_wstream_kernel
    if residual is not None:
        in_specs.append(pl.BlockSpec((tm, tn), lambda j, i: (i, j)))
        args.append(residual)
        body = _mm_wstream_residual_kernel
    return pl.pallas_call(
        functools.partial(body, layer=layer, n_chunks=n_chunks),
        out_shape=jax.ShapeDtypeStruct((m, n), out_dtype),
        grid=(n // tn, n_chunks),
        in_specs=in_specs,
        out_specs=pl.BlockSpec((tm, tn), lambda j, i: (i, j)),
        scratch_shapes=_weight_stream_scratch(k, tn, n_chunks),
        compiler_params=_cparams(("arbitrary", "arbitrary")),
        name=name,
    )(*args)


SMALL_M_TK = 512


def _mm_small_m_kernel(a_ref, w_ref, o_ref, acc_ref):
    k = pl.program_id(0)

    @pl.when(k == 0)
    def _():
        acc_ref[...] = jnp.zeros_like(acc_ref)

    acc_ref[...] += jnp.dot(a_ref[...], w_ref[...].astype(BF16), preferred_element_type=F32)

    @pl.when(k == pl.num_programs(0) - 1)
    def _():
        o_ref[...] = acc_ref[...].astype(o_ref.dtype)


def matmul_few_rows(a, w_stack, layer, out_dtype, name):
    m, k = a.shape
    n = w_stack.shape[-1]
    tk = SMALL_M_TK
    return pl.pallas_call(
        _mm_small_m_kernel,
        out_shape=jax.ShapeDtypeStruct((m, n), out_dtype),
        grid=(k // tk,),
        in_specs=[pl.BlockSpec((m, tk), lambda c: (0, c)),
                  pl.BlockSpec((None, tk, n), lambda c: (layer, c, 0))],
        out_specs=pl.BlockSpec((m, n), lambda c: (0, 0)),
        scratch_shapes=[pltpu.VMEM((m, n), F32)],
        compiler_params=_cparams(("arbitrary",)),
        name=name,
    )(a, w_stack)


MOBA_TQ = MOBA_BLOCK


ALIBI_TERMS = 3


def _alibi_key_columns(slope, first_col, base):
    pos = lax.broadcasted_iota(jnp.int32, (SEQ, HEAD_DIM), 0) - SEQ // 2
    lane = lax.broadcasted_iota(jnp.int32, (SEQ, HEAD_DIM), 1)
    rest = (slope * HEAD_DIM ** 0.5) * pos.astype(F32)
    cols = base
    for term in range(ALIBI_TERMS):
        piece = rest.astype(BF16).astype(F32)
        rest = rest - piece
        cols = jnp.where(lane == first_col + term, piece, cols)
    return cols


def _alibi_query_columns(first_col, base):
    lane = lax.broadcasted_iota(jnp.int32, base.shape, 1)
    return jnp.where(lane < first_col, base, jnp.where(lane < first_col + ALIBI_TERMS, 1.0, base))


def _moba_kernel(slopes_ref, q_ref, k_ref, v_ref, z_ref, o_ref, qa_ref, ka_ref, va_ref):
    h = pl.program_id(1)
    nb, blk, hd = N_MOBA_BLOCKS, MOBA_BLOCK, HEAD_DIM

    k = k_ref[...]
    ka_ref[:, :hd] = k
    blk_shift = blk.bit_length() - 1
    key_blk = lax.broadcasted_iota(jnp.int32, (SEQ, hd), 0) >> blk_shift
    block_onehot = jnp.where(key_blk == lax.broadcasted_iota(jnp.int32, (SEQ, hd), 1), 1.0, 0.0)
    ka_ref[:, hd:] = _alibi_key_columns(slopes_ref[B_HEADS + h], nb, block_onehot).astype(BF16)
    va_ref[:, :hd] = v_ref[...]
    va_ref[:, hd:] = jnp.ones((SEQ, hd), BF16)

    km = jnp.concatenate(
        [jnp.mean(k[n * blk:(n + 1) * blk, :].astype(F32), axis=0, keepdims=True) for n in range(nb)], axis=0)
    q = q_ref[...]
    gate_t = lax.dot_general(km, q.astype(F32), NT_DIMS, precision=lax.Precision.HIGHEST,
                             preferred_element_type=F32)
    n_idx = lax.broadcasted_iota(jnp.int32, (nb, SEQ), 0)
    q_blk = lax.broadcasted_iota(jnp.int32, (nb, SEQ), 1) >> blk_shift
    past = n_idx < q_blk
    g = jnp.where(past, gate_t, NEG)
    rank = jnp.zeros((nb, SEQ), jnp.int32)
    for m in range(nb):
        row = g[m:m + 1, :]
        rank = rank + jnp.where(row > g, 1, jnp.where(row == g, jnp.where(n_idx > m, 1, 0), 0))
    mask_t = jnp.where(past, jnp.where(rank < MOBA_TOPK, 0.0, NEG),
                       jnp.where(n_idx == q_blk, 0.0, NEG))
    mask_rows = jnp.concatenate([mask_t, jnp.zeros((hd - nb, SEQ), F32)], axis=0).T
    qa_ref[:, :hd] = q
    qa_ref[:, hd:] = _alibi_query_columns(nb, mask_rows).astype(BF16)

    c1 = (hd ** -0.5) * LOG2E
    r_idx = lax.broadcasted_iota(jnp.int32, (MOBA_TQ, blk), 0)
    c_idx = lax.broadcasted_iota(jnp.int32, (MOBA_TQ, blk), 1)
    causal = r_idx >= c_idx

    def raw_scores(i):
        return lax.dot_general(qa_ref[i * MOBA_TQ:(i + 1) * MOBA_TQ, :], ka_ref[:(i + 1) * blk, :], NT_DIMS,
                               preferred_element_type=F32)

    s_next = raw_scores(0)
    for i in range(nb):
        rows = slice(i * MOBA_TQ, (i + 1) * MOBA_TQ)
        width = (i + 1) * blk
        s = s_next
        if i + 1 < nb:
            s_next = raw_scores(i + 1)
        s_own = jnp.where(causal, s[:, i * blk:], NEG)
        s = s_own if i == 0 else jnp.concatenate([s[:, :i * blk], s_own], axis=1)
        p = jnp.exp2((s - jnp.max(s, axis=-1, keepdims=True)) * c1)
        oa = jnp.dot(p.astype(BF16), va_ref[:width, :], preferred_element_type=F32)
        out = oa[:, :hd] / oa[:, hd:]
        o_ref[rows, :] = (out * jax.nn.silu(z_ref[rows, :].astype(F32))).astype(o_ref.dtype)


def moba_branch(proj, slopes):
    hpb = BW // HEAD_DIM
    seg = lambda c: pl.BlockSpec((SEQ, HEAD_DIM), lambda b, h: (b, c * hpb + h))
    return pl.pallas_call(
        _moba_kernel,
        out_shape=jax.ShapeDtypeStruct((ROWS, BW), BF16),
        grid=(BATCH, A_HEADS),
        in_specs=[pl.BlockSpec(memory_space=pltpu.SMEM), seg(COL_QA), seg(COL_KA), seg(COL_VA), seg(COL_ZA)],
        out_specs=pl.BlockSpec((SEQ, HEAD_DIM), lambda b, h: (b, h)),
        scratch_shapes=[pltpu.VMEM((SEQ, 2 * HEAD_DIM), BF16),
                        pltpu.VMEM((SEQ, 2 * HEAD_DIM), BF16),
                        pltpu.VMEM((SEQ, 2 * HEAD_DIM), BF16)],
        compiler_params=_cparams(("parallel", "parallel")),
        name="moba_branch",
    )(slopes, proj, proj, proj, proj)


DIFF_TQ = 256


def _diff_kernel(slopes_ref, lq1_ref, lk1_ref, lq2_ref, lk2_ref, subg_ref, q_ref, k_ref, v_ref, z_ref,
                 o_ref, qa_ref, ka_ref, *, lam_init):
    h = pl.program_id(1)
    tq, hd = DIFF_TQ, HEAD_DIM
    c1 = (hd ** -0.5) * LOG2E
    zeros = jnp.zeros((SEQ, hd), F32)
    q_aug = _alibi_query_columns(0, zeros).astype(BF16)
    k_aug = _alibi_key_columns(slopes_ref[h], 0, zeros).astype(BF16)
    for mp in range(2):
        qa_ref[:, 2 * mp * hd:(2 * mp + 1) * hd] = q_ref[:, mp * hd:(mp + 1) * hd]
        qa_ref[:, (2 * mp + 1) * hd:(2 * mp + 2) * hd] = q_aug
        ka_ref[:, 2 * mp * hd:(2 * mp + 1) * hd] = k_ref[:, mp * hd:(mp + 1) * hd]
        ka_ref[:, (2 * mp + 1) * hd:(2 * mp + 2) * hd] = k_aug
    lam = (jnp.exp(jnp.sum(lq1_ref[...] * lk1_ref[...], axis=-1, keepdims=True))
           - jnp.exp(jnp.sum(lq2_ref[...] * lk2_ref[...], axis=-1, keepdims=True)) + lam_init)
    r_idx = lax.broadcasted_iota(jnp.int32, (tq, tq), 0)
    c_idx = lax.broadcasted_iota(jnp.int32, (tq, tq), 1)
    causal = r_idx >= c_idx
    n_tiles = SEQ // tq

    def raw_scores(i, mp):
        cols = slice(2 * mp * hd, (2 * mp + 2) * hd)
        return lax.dot_general(qa_ref[i * tq:(i + 1) * tq, cols], ka_ref[:(i + 1) * tq, cols], NT_DIMS,
                               preferred_element_type=F32)

    def softmax_numerator(s, i):
        s_own = jnp.where(causal, s[:, i * tq:], NEG)
        s = s_own if i == 0 else jnp.concatenate([s[:, :i * tq], s_own], axis=1)
        p = jnp.exp2((s - jnp.max(s, axis=-1, keepdims=True)) * c1)
        return p.astype(BF16), jnp.sum(p, axis=-1, keepdims=True)

    s0_next = raw_scores(0, 0)
    for i in range(n_tiles):
        rows = slice(i * tq, (i + 1) * tq)
        s0, s1 = s0_next, raw_scores(i, 1)
        p0, l0 = softmax_numerator(s0, i)
        if i + 1 < n_tiles:
            s0_next = raw_scores(i + 1, 0)
        p1, l1 = softmax_numerator(s1, i)
        o = jnp.dot(jnp.concatenate([p0, p1], axis=0), v_ref[:(i + 1) * tq, :], preferred_element_type=F32)
        ob = o[:tq] / l0 - lam * (o[tq:] / l1)
        ms = jnp.mean(ob * ob, axis=-1, keepdims=True)
        ob = ob * lax.rsqrt(ms + RMS_EPS) * subg_ref[...]
        ob = ob * (1.0 - lam_init)
        o_ref[rows, :] = (ob * jax.nn.silu(z_ref[rows, :].astype(F32))).astype(o_ref.dtype)


def diff_branch(proj, slopes, lq1, lk1, lq2, lk2, subln_g, lam_init):
    hpb = BW // B_V_DIM
    vec = lambda a: a.reshape(1, -1).astype(F32)
    small = lambda n: pl.BlockSpec((1, n), lambda b, h: (0, 0))
    seg = lambda c: pl.BlockSpec((SEQ, B_V_DIM), lambda b, h: (b, c * hpb + h))
    return pl.pallas_call(
        functools.partial(_diff_kernel, lam_init=lam_init),
        out_shape=jax.ShapeDtypeStruct((ROWS, BW), BF16),
        grid=(BATCH, B_HEADS),
        in_specs=[pl.BlockSpec(memory_space=pltpu.SMEM),
                  small(HEAD_DIM), small(HEAD_DIM), small(HEAD_DIM), small(HEAD_DIM), small(B_V_DIM),
                  seg(COL_QB), seg(COL_KB), seg(COL_VB), seg(COL_ZB)],
        out_specs=pl.BlockSpec((SEQ, B_V_DIM), lambda b, h: (b, h)),
        scratch_shapes=[pltpu.VMEM((SEQ, 4 * HEAD_DIM), BF16), pltpu.VMEM((SEQ, 4 * HEAD_DIM), BF16)],
        compiler_params=_cparams(("parallel", "parallel")),
        name="diff_branch",
    )(slopes, vec(lq1), vec(lk1), vec(lq2), vec(lk2), vec(subln_g), proj, proj, proj, proj)


SGU_TILE_CHUNKS = 4


def _sgu_kernel(u_ref, v_ref, z_ref, lng_ref, lnb_ref, w_ref, bt_ref, o_ref):
    v = jax.nn.gelu(v_ref[...].astype(F32))
    mu = jnp.mean(v, axis=-1, keepdims=True)
    var = jnp.mean(jnp.square(v - mu), axis=-1, keepdims=True)
    vn = ((v - mu) * lax.rsqrt(var + LN_EPS) * lng_ref[...] + lnb_ref[...]).astype(BF16)
    r_idx = lax.broadcasted_iota(jnp.int32, (SGU_CHUNK, SGU_CHUNK), 0)
    c_idx = lax.broadcasted_iota(jnp.int32, (SGU_CHUNK, SGU_CHUNK), 1)
    causal = r_idx >= c_idx
    chunk_rows = [slice(c * SGU_CHUNK, (c + 1) * SGU_CHUNK) for c in range(SGU_TILE_CHUNKS)]
    for g in range(SGU_GROUPS):
        cols = slice(g * SGU_GROUP_CH, (g + 1) * SGU_GROUP_CH)
        w = jnp.where(causal, w_ref[g], 0.0).astype(BF16)
        vn_g = jnp.concatenate([vn[rows, cols] for rows in chunk_rows], axis=1)
        mixed_g = jnp.dot(w, vn_g, preferred_element_type=F32) + bt_ref[:, g:g + 1]
        for c, rows in enumerate(chunk_rows):
            mixed = mixed_g[:, c * SGU_GROUP_CH:(c + 1) * SGU_GROUP_CH]
            u = jax.nn.gelu(u_ref[rows, cols].astype(F32))
            o_ref[rows, cols] = (u * mixed * jax.nn.silu(z_ref[rows, cols].astype(F32))).astype(o_ref.dtype)


def sgu_branch(proj, ln_g, ln_b, w_s, b_s):
    tile = SGU_TILE_CHUNKS * SGU_CHUNK
    seg = lambda c: pl.BlockSpec((tile, BW), lambda t: (t, c))
    return pl.pallas_call(
        _sgu_kernel,
        out_shape=jax.ShapeDtypeStruct((ROWS, BW), BF16),
        grid=(ROWS // tile,),
        in_specs=[seg(COL_UC), seg(COL_VC), seg(COL_ZC),
                  pl.BlockSpec((1, BW), lambda t: (0, 0)),
                  pl.BlockSpec((1, BW), lambda t: (0, 0)),
                  pl.BlockSpec((SGU_GROUPS, SGU_CHUNK, SGU_CHUNK), lambda t: (0, 0, 0)),
                  pl.BlockSpec((SGU_CHUNK, SGU_GROUPS), lambda t: (0, 0))],
        out_specs=pl.BlockSpec((tile, BW), lambda t: (t, 0)),
        compiler_params=_cparams(("parallel",)),
        name="sgu_branch",
    )(proj, proj, proj, ln_g.reshape(1, BW), ln_b.reshape(1, BW), w_s, b_s.T)


MEM_TQ = 1024


def _mem_kernel(q_ref, kv_ref, z_ref, o_ref):
    c1 = (M_HEAD_DIM ** -0.5) * LOG2E
    for h in range(M_HEADS):
        cols = slice(h * M_HEAD_DIM, (h + 1) * M_HEAD_DIM)
        k = kv_ref[:, cols]
        v = kv_ref[:, BW + h * M_HEAD_DIM:BW + (h + 1) * M_HEAD_DIM]
        for i in range(SEQ // MEM_TQ):
            rows = slice(i * MEM_TQ, (i + 1) * MEM_TQ)
            t = lax.dot_general(q_ref[rows, cols], k, NT_DIMS, preferred_element_type=F32) * c1
            p = jnp.exp2(t - jnp.max(t, axis=-1, keepdims=True))
            l = jnp.sum(p, axis=-1, keepdims=True)
            out = jnp.dot(p.astype(BF16), v, preferred_element_type=F32) / l
            o_ref[rows, cols] = (out * jax.nn.silu(z_ref[rows, cols].astype(F32))).astype(o_ref.dtype)


def mem_branch(proj, mem_kv):
    return pl.pallas_call(
        _mem_kernel,
        out_shape=jax.ShapeDtypeStruct((ROWS, BW), BF16),
        grid=(BATCH,),
        in_specs=[pl.BlockSpec((SEQ, BW), lambda b: (b, COL_QM)),
                  pl.BlockSpec((MEM_LEN, 2 * BW), lambda b: (b, 0)),
                  pl.BlockSpec((SEQ, BW), lambda b: (b, COL_ZM))],
        out_specs=pl.BlockSpec((SEQ, BW), lambda b: (b, 0)),
        compiler_params=_cparams(("parallel",)),
        name="mem_branch",
    )(proj, mem_kv, proj)


MERGE_TM = 512
MERGE_TN = 1024


def _merge_kernel(ya_ref, yb_ref, yc_ref, ym_ref, w_hbm, g0_ref, g1_ref, g2_ref, g3_ref, o_ref,
                  wb_ref, stage_ref, sem, *, layer, n_chunks):
    stream = _WeightStream(w_hbm, wb_ref, stage_ref, sem, layer, o_ref.shape[1])
    w = stream.before_matmul(n_chunks)
    ys = (ya_ref, yb_ref, yc_ref, ym_ref)
    gs = (g0_ref, g1_ref, g2_ref, g3_ref)
    merged = None
    for br in range(N_BRANCH):
        gate = jax.nn.sigmoid(gs[br][...].astype(F32))
        term = gate * jnp.dot(ys[br][...], w[br * BW:(br + 1) * BW, :], preferred_element_type=F32)
        merged = term if merged is None else merged + term
    o_ref[...] = merged.astype(o_ref.dtype)


def gated_merge(ya, yb, yc, ym, w_branch, layer, proj):
    tm, tn = MERGE_TM, MERGE_TN
    n_chunks = ROWS // tm
    k = N_BRANCH * BW
    y_spec = pl.BlockSpec((tm, BW), lambda j, i: (i, 0))
    gate_spec = lambda br: pl.BlockSpec(
        (tm, tn), lambda j, i: (i, (COL_GATE * BW + br * D_MODEL) // tn + j))
    return pl.pallas_call(
        functools.partial(_merge_kernel, layer=layer, n_chunks=n_chunks),
        out_shape=jax.ShapeDtypeStruct((ROWS, D_MODEL), BF16),
        grid=(D_MODEL // tn, n_chunks),
        in_specs=[y_spec, y_spec, y_spec, y_spec,
                  pl.BlockSpec(memory_space=pl.ANY),
                  gate_spec(0), gate_spec(1), gate_spec(2), gate_spec(3)],
        out_specs=pl.BlockSpec((tm, tn), lambda j, i: (i, j)),
        scratch_shapes=_weight_stream_scratch(k, tn, n_chunks),
        compiler_params=_cparams(("arbitrary", "arbitrary")),
        name="gated_merge",
    )(ya, yb, yc, ym, w_branch.reshape(DEPTH, k, D_MODEL), proj, proj, proj, proj)


def kernel(x, mem, norm_g, w_in, mem_norm_g, w_mem_kv, diff_lam_q1, diff_lam_k1, diff_lam_q2, diff_lam_k2,
           diff_subln_g, sgu_ln_g, sgu_ln_b, sgu_w, sgu_b, w_branch, w_out, final_g):
    xf = x.reshape(ROWS, D_MODEL)
    memf = mem.reshape(BATCH * MEM_LEN, D_MODEL)
    slopes = jnp.exp2(-8.0 * jnp.arange(1, N_ALIBI_HEADS + 1, dtype=F32) / N_ALIBI_HEADS)
    for l in range(DEPTH):
        lam_init = 0.8 - 0.6 * math.exp(-0.3 * l)
        h = rmsnorm(xf, norm_g[l], BF16)
        proj = matmul_f32_weights(h, w_in, l, BF16, tm=1024, tn=1024, name="in_proj")
        hm = rmsnorm(memf, mem_norm_g[l], BF16)
        mem_kv = matmul_few_rows(hm, w_mem_kv, l, BF16, name="mem_kv_proj")
        ya = moba_branch(proj, slopes)
        yb = diff_branch(proj, slopes, diff_lam_q1[l], diff_lam_k1[l], diff_lam_q2[l], diff_lam_k2[l],
                         diff_subln_g[l], lam_init)
        yc = sgu_branch(proj, sgu_ln_g[l], sgu_ln_b[l], sgu_w[l], sgu_b[l])
        ym = mem_branch(proj, mem_kv)
        merged = gated_merge(ya, yb, yc, ym, w_branch, l, proj)
        xf = matmul_f32_weights(merged, w_out, l, F32, tm=512, tn=1024, residual=xf, name="out_proj")
    out = rmsnorm(xf, final_g, F32)
    return out.reshape(BATCH, SEQ, D_MODEL)
```

```python
import functools
import math

import jax
import jax.numpy as jnp
from jax import lax
from jax.experimental import pallas as pl
from jax.experimental.pallas import tpu as pltpu

D_MODEL = 4096
BATCH = 4
SEQ = 2048
DEPTH = 2
MEM_LEN = 256
HEAD_DIM = 128
BW = D_MODEL // 4
N_BRANCH = 4
A_HEADS = BW // HEAD_DIM
MOBA_BLOCK = 256
MOBA_TOPK = 3
B_HEADS = BW // (2 * HEAD_DIM)
B_V_DIM = 2 * HEAD_DIM
SGU_CHUNK = 128
SGU_GROUP_CH = 128
SGU_GROUPS = BW // SGU_GROUP_CH
M_HEADS = 4
M_HEAD_DIM = BW // M_HEADS
N_ALIBI_HEADS = A_HEADS + B_HEADS
RMS_EPS = 1e-6
LN_EPS = 1e-5
NEG = -1e30
N_MOBA_BLOCKS = SEQ // MOBA_BLOCK

(COL_QA, COL_KA, COL_VA, COL_ZA, COL_QB, COL_KB, COL_VB, COL_ZB,
 COL_UC, COL_VC, COL_ZC, COL_QM, COL_ZM, COL_GATE) = range(14)

ROWS = BATCH * SEQ
F32 = jnp.float32
BF16 = jnp.bfloat16
LOG2E = math.log2(math.e)

VMEM_LIMIT_BYTES = 58 * 1024 * 1024

NT_DIMS = (((1,), (1,)), ((), ()))


def _cparams(semantics):
    return pltpu.CompilerParams(dimension_semantics=semantics, vmem_limit_bytes=VMEM_LIMIT_BYTES)


def _rmsnorm_kernel(x_ref, g_ref, o_ref):
    x = x_ref[...].astype(F32)
    ms = jnp.mean(x * x, axis=-1, keepdims=True)
    o_ref[...] = (x * lax.rsqrt(ms + RMS_EPS) * g_ref[...]).astype(o_ref.dtype)


def rmsnorm(x, g, out_dtype, tile_rows=512):
    rows, d = x.shape
    return pl.pallas_call(
        _rmsnorm_kernel,
        out_shape=jax.ShapeDtypeStruct((rows, d), out_dtype),
        grid=(rows // tile_rows,),
        in_specs=[pl.BlockSpec((tile_rows, d), lambda i: (i, 0)),
                  pl.BlockSpec((1, d), lambda i: (0, 0))],
        out_specs=pl.BlockSpec((tile_rows, d), lambda i: (i, 0)),
        compiler_params=_cparams(("parallel",)),
        name="rmsnorm",
    )(x, g.reshape(1, d).astype(F32))


STAGE_SLOTS = 4


class _WeightStream:
    def __init__(self, w_hbm, wb_ref, stage_ref, sem, layer, tn):
        self.w_hbm, self.wb_ref, self.stage_ref, self.sem = w_hbm, wb_ref, stage_ref, sem
        self.layer, self.tn = layer, tn
        self.j, self.i = pl.program_id(0), pl.program_id(1)
        self.nj, self.ni = pl.num_programs(0), pl.num_programs(1)
        self.chunk_rows = stage_ref.shape[1]

    def _rows(self, chunk):
        return pl.ds(pl.multiple_of(chunk * self.chunk_rows, self.chunk_rows), self.chunk_rows)

    def _copy(self, block, chunk):
        cols = pl.ds(pl.multiple_of(block * self.tn, self.tn), self.tn)
        src = self.w_hbm.at[self.layer, self._rows(chunk), cols]
        slot = chunk % STAGE_SLOTS
        return pltpu.make_async_copy(src, self.stage_ref.at[slot], self.sem.at[slot])

    def _round(self, wb_slot, chunk):
        self.wb_ref[wb_slot, self._rows(chunk), :] = self.stage_ref[chunk % STAGE_SLOTS].astype(BF16)

    def _prefetch_block(self, j):
        return jnp.minimum(j + 1, self.nj - 1)

    def before_matmul(self, n_chunks):
        j, i = self.j, self.i
        step = j * self.ni + i

        @pl.when(step == 0)
        def _():
            depth = STAGE_SLOTS - 1
            for c in range(depth):
                self._copy(0, c).start()
            for c in range(n_chunks):
                if c + depth < n_chunks:
                    self._copy(0, c + depth).start()
                self._copy(0, c).wait()
                self._round(0, c)
            self._copy(self._prefetch_block(0), 0).start()

        self._copy(self._prefetch_block(j), i).wait()

        @pl.when(step + 1 < self.nj * self.ni)
        def _():
            wrap = i + 1 == self.ni
            i_next = jnp.where(wrap, 0, i + 1)
            j_next = jnp.where(wrap, j + 1, j)
            self._copy(self._prefetch_block(j_next), i_next).start()

        self._round((j + 1) % 2, i)
        return self.wb_ref.at[j % 2]


def _mm_wstream_kernel(a_ref, w_hbm, o_ref, wb_ref, stage_ref, sem, *, layer, n_chunks):
    stream = _WeightStream(w_hbm, wb_ref, stage_ref, sem, layer, o_ref.shape[1])
    w = stream.before_matmul(n_chunks)
    o_ref[...] = jnp.dot(a_ref[...], w[...], preferred_element_type=F32).astype(o_ref.dtype)


def _mm_wstream_residual_kernel(a_ref, w_hbm, r_ref, o_ref, wb_ref, stage_ref, sem, *, layer, n_chunks):
    stream = _WeightStream(w_hbm, wb_ref, stage_ref, sem, layer, o_ref.shape[1])
    w = stream.before_matmul(n_chunks)
    acc = jnp.dot(a_ref[...], w[...], preferred_element_type=F32)
    o_ref[...] = (r_ref[...] + acc).astype(o_ref.dtype)


def _weight_stream_scratch(k, tn, n_chunks):
    assert n_chunks % STAGE_SLOTS == 0
    return [pltpu.VMEM((2, k, tn), BF16),
            pltpu.VMEM((STAGE_SLOTS, k // n_chunks, tn), F32),
            pltpu.SemaphoreType.DMA((STAGE_SLOTS,))]


def matmul_f32_weights(a, w_stack, layer, out_dtype, tm, tn, residual=None, name="matmul_w"):
    m, k = a.shape
    n = w_stack.shape[-1]
    n_chunks = m // tm
    in_specs = [pl.BlockSpec((tm, k), lambda j, i: (i, 0)),
                pl.BlockSpec(memory_space=pl.ANY)]
    args = [a, w_stack]
    body = _mm_wstream_kernel
    if residual is not None:
        in_specs.append(pl.BlockSpec((tm, tn), lambda j, i: (i, j)))
        args.append(residual)
        body = _mm_wstream_residual_kernel
    return pl.pallas_call(
        functools.partial(body, layer=layer, n_chunks=n_chunks),
        out_shape=jax.ShapeDtypeStruct((m, n), out_dtype),
        grid=(n // tn, n_chunks),
        in_specs=in_specs,
        out_specs=pl.BlockSpec((tm, tn), lambda j, i: (i, j)),
        scratch_shapes=_weight_stream_scratch(k, tn, n_chunks),
        compiler_params=_cparams(("arbitrary", "arbitrary")),
        name=name,
    )(*args)


SMALL_M_TK = 512


def _rmsnorm_mm_small_m_kernel(x_ref, g_ref, w_ref, o_ref, acc_ref, ss_ref, *, d):
    k = pl.program_id(0)

    @pl.when(k == 0)
    def _():
        acc_ref[...] = jnp.zeros_like(acc_ref)
        ss_ref[...] = jnp.zeros_like(ss_ref)

    x = x_ref[...]
    ss_ref[...] += jnp.sum(x * x, axis=-1, keepdims=True)
    a = (x * g_ref[...]).astype(BF16)
    acc_ref[...] += jnp.dot(a, w_ref[...].astype(BF16), preferred_element_type=F32)

    @pl.when(k == pl.num_programs(0) - 1)
    def _():
        o_ref[...] = (acc_ref[...] * lax.rsqrt(ss_ref[...] / d + RMS_EPS)).astype(o_ref.dtype)


def rmsnorm_matmul_few_rows(x, g, w_stack, layer, out_dtype, name):
    m, k = x.shape
    n = w_stack.shape[-1]
    tk = SMALL_M_TK
    return pl.pallas_call(
        functools.partial(_rmsnorm_mm_small_m_kernel, d=k),
        out_shape=jax.ShapeDtypeStruct((m, n), out_dtype),
        grid=(k // tk,),
        in_specs=[pl.BlockSpec((m, tk), lambda c: (0, c)),
                  pl.BlockSpec((1, tk), lambda c: (0, c)),
                  pl.BlockSpec((None, tk, n), lambda c: (layer, c, 0))],
        out_specs=pl.BlockSpec((m, n), lambda c: (0, 0)),
        scratch_shapes=[pltpu.VMEM((m, n), F32), pltpu.VMEM((m, 1), F32)],
        compiler_params=_cparams(("arbitrary",)),
        name=name,
    )(x, g.reshape(1, k).astype(F32), w_stack)


MOBA_TQ = MOBA_BLOCK


ALIBI_TERMS = 3


def _alibi_key_columns(slope, first_col, base):
    pos = lax.broadcasted_iota(jnp.int32, (SEQ, HEAD_DIM), 0) - SEQ // 2
    lane = lax.broadcasted_iota(jnp.int32, (SEQ, HEAD_DIM), 1)
    rest = (slope * HEAD_DIM ** 0.5) * pos.astype(F32)
    cols = base
    for term in range(ALIBI_TERMS):
        piece = rest.astype(BF16).astype(F32)
        rest = rest - piece
        cols = jnp.where(lane == first_col + term, piece, cols)
    return cols


def _alibi_query_columns(first_col, base):
    lane = lax.broadcasted_iota(jnp.int32, base.shape, 1)
    return jnp.where(lane < first_col, base, jnp.where(lane < first_col + ALIBI_TERMS, 1.0, base))


def _moba_kernel(slopes_ref, q_ref, k_ref, v_ref, z_ref, o_ref, qa_ref, ka_ref, va_ref):
    h = pl.program_id(1)
    nb, blk, hd = N_MOBA_BLOCKS, MOBA_BLOCK, HEAD_DIM

    k = k_ref[...]
    ka_ref[:, :hd] = k
    blk_shift = blk.bit_length() - 1
    key_blk = lax.broadcasted_iota(jnp.int32, (SEQ, hd), 0) >> blk_shift
    block_onehot = jnp.where(key_blk == lax.broadcasted_iota(jnp.int32, (SEQ, hd), 1), 1.0, 0.0)
    ka_ref[:, hd:] = _alibi_key_columns(slopes_ref[B_HEADS + h], nb, block_onehot).astype(BF16)
    va_ref[:, :hd] = v_ref[...]
    va_ref[:, hd:] = jnp.ones((SEQ, hd), BF16)

    km = jnp.concatenate(
        [jnp.mean(k[n * blk:(n + 1) * blk, :].astype(F32), axis=0, keepdims=True) for n in range(nb)], axis=0)
    q = q_ref[...]
    gate_t = lax.dot_general(km, q.astype(F32), NT_DIMS, precision=lax.Precision.HIGHEST,
                             preferred_element_type=F32)
    n_idx = lax.broadcasted_iota(jnp.int32, (nb, SEQ), 0)
    q_blk = lax.broadcasted_iota(jnp.int32, (nb, SEQ), 1) >> blk_shift
    past = n_idx < q_blk
    g = jnp.where(past, gate_t, NEG)
    rank = jnp.zeros((nb, SEQ), jnp.int32)
    for m in range(nb):
        row = g[m:m + 1, :]
        rank = rank + jnp.where(row > g, 1, jnp.where(row == g, jnp.where(n_idx > m, 1, 0), 0))
    mask_t = jnp.where(past, jnp.where(rank < MOBA_TOPK, 0.0, NEG),
                       jnp.where(n_idx == q_blk, 0.0, NEG))
    mask_rows = jnp.concatenate([mask_t, jnp.zeros((hd - nb, SEQ), F32)], axis=0).T
    qa_ref[:, :hd] = q
    qa_ref[:, hd:] = _alibi_query_columns(nb, mask_rows).astype(BF16)

    c1 = (hd ** -0.5) * LOG2E
    r_idx = lax.broadcasted_iota(jnp.int32, (MOBA_TQ, blk), 0)
    c_idx = lax.broadcasted_iota(jnp.int32, (MOBA_TQ, blk), 1)
    causal = r_idx >= c_idx

    def raw_scores(i):
        return lax.dot_general(qa_ref[i * MOBA_TQ:(i + 1) * MOBA_TQ, :], ka_ref[:(i + 1) * blk, :], NT_DIMS,
                               preferred_element_type=F32)

    s_next = raw_scores(0)
    for i in range(nb):
        rows = slice(i * MOBA_TQ, (i + 1) * MOBA_TQ)
        width = (i + 1) * blk
        s = s_next
        if i + 1 < nb:
            s_next = raw_scores(i + 1)
        s_own = jnp.where(causal, s[:, i * blk:], NEG)
        s = s_own if i == 0 else jnp.concatenate([s[:, :i * blk], s_own], axis=1)
        p = jnp.exp2((s - jnp.max(s, axis=-1, keepdims=True)) * c1)
        oa = jnp.dot(p.astype(BF16), va_ref[:width, :], preferred_element_type=F32)
        out = oa[:, :hd] / oa[:, hd:]
        o_ref[rows, :] = (out * jax.nn.silu(z_ref[rows, :].astype(F32))).astype(o_ref.dtype)


def moba_branch(proj, slopes):
    hpb = BW // HEAD_DIM
    seg = lambda c: pl.BlockSpec((SEQ, HEAD_DIM), lambda b, h: (b, c * hpb + h))
    return pl.pallas_call(
        _moba_kernel,
        out_shape=jax.ShapeDtypeStruct((ROWS, BW), BF16),
        grid=(BATCH, A_HEADS),
        in_specs=[pl.BlockSpec(memory_space=pltpu.SMEM), seg(COL_QA), seg(COL_KA), seg(COL_VA), seg(COL_ZA)],
        out_specs=pl.BlockSpec((SEQ, HEAD_DIM), lambda b, h: (b, h)),
        scratch_shapes=[pltpu.VMEM((SEQ, 2 * HEAD_DIM), BF16),
                        pltpu.VMEM((SEQ, 2 * HEAD_DIM), BF16),
                        pltpu.VMEM((SEQ, 2 * HEAD_DIM), BF16)],
        compiler_params=_cparams(("parallel", "parallel")),
        name="moba_branch",
    )(slopes, proj, proj, proj, proj)


DIFF_TQ = 256


def _diff_kernel(slopes_ref, lq1_ref, lk1_ref, lq2_ref, lk2_ref, subg_ref, q_ref, k_ref, v_ref, z_ref,
                 o_ref, qa_ref, ka_ref, *, lam_init):
    h = pl.program_id(1)
    tq, hd = DIFF_TQ, HEAD_DIM
    c1 = (hd ** -0.5) * LOG2E
    zeros = jnp.zeros((SEQ, hd), F32)
    q_aug = _alibi_query_columns(0, zeros).astype(BF16)
    k_aug = _alibi_key_columns(slopes_ref[h], 0, zeros).astype(BF16)
    for mp in range(2):
        qa_ref[:, 2 * mp * hd:(2 * mp + 1) * hd] = q_ref[:, mp * hd:(mp + 1) * hd]
        qa_ref[:, (2 * mp + 1) * hd:(2 * mp + 2) * hd] = q_aug
        ka_ref[:, 2 * mp * hd:(2 * mp + 1) * hd] = k_ref[:, mp * hd:(mp + 1) * hd]
        ka_ref[:, (2 * mp + 1) * hd:(2 * mp + 2) * hd] = k_aug
    lam = (jnp.exp(jnp.sum(lq1_ref[...] * lk1_ref[...], axis=-1, keepdims=True))
           - jnp.exp(jnp.sum(lq2_ref[...] * lk2_ref[...], axis=-1, keepdims=True)) + lam_init)
    r_idx = lax.broadcasted_iota(jnp.int32, (tq, tq), 0)
    c_idx = lax.broadcasted_iota(jnp.int32, (tq, tq), 1)
    causal = r_idx >= c_idx
    n_tiles = SEQ // tq

    def raw_scores(i, mp):
        cols = slice(2 * mp * hd, (2 * mp + 2) * hd)
        return lax.dot_general(qa_ref[i * tq:(i + 1) * tq, cols], ka_ref[:(i + 1) * tq, cols], NT_DIMS,
                               preferred_element_type=F32)

    def softmax_numerator(s, i):
        s_own = jnp.where(causal, s[:, i * tq:], NEG)
        s = s_own if i == 0 else jnp.concatenate([s[:, :i * tq], s_own], axis=1)
        p = jnp.exp2((s - jnp.max(s, axis=-1, keepdims=True)) * c1)
        return p.astype(BF16), jnp.sum(p, axis=-1, keepdims=True)

    s0_next = raw_scores(0, 0)
    for i in range(n_tiles):
        rows = slice(i * tq, (i + 1) * tq)
        s0, s1 = s0_next, raw_scores(i, 1)
        p0, l0 = softmax_numerator(s0, i)
        if i + 1 < n_tiles:
            s0_next = raw_scores(i + 1, 0)
        p1, l1 = softmax_numerator(s1, i)
        o = jnp.dot(jnp.concatenate([p0, p1], axis=0), v_ref[:(i + 1) * tq, :], preferred_element_type=F32)
        ob = o[:tq] / l0 - lam * (o[tq:] / l1)
        ms = jnp.mean(ob * ob, axis=-1, keepdims=True)
        ob = ob * lax.rsqrt(ms + RMS_EPS) * subg_ref[...]
        ob = ob * (1.0 - lam_init)
        o_ref[rows, :] = (ob * jax.nn.silu(z_ref[rows, :].astype(F32))).astype(o_ref.dtype)


def diff_branch(proj, slopes, lq1, lk1, lq2, lk2, subln_g, lam_init):
    hpb = BW // B_V_DIM
    vec = lambda a: a.reshape(1, -1).astype(F32)
    small = lambda n: pl.BlockSpec((1, n), lambda b, h: (0, 0))
    seg = lambda c: pl.BlockSpec((SEQ, B_V_DIM), lambda b, h: (b, c * hpb + h))
    return pl.pallas_call(
        functools.partial(_diff_kernel, lam_init=lam_init),
        out_shape=jax.ShapeDtypeStruct((ROWS, BW), BF16),
        grid=(BATCH, B_HEADS),
        in_specs=[pl.BlockSpec(memory_space=pltpu.SMEM),
                  small(HEAD_DIM), small(HEAD_DIM), small(HEAD_DIM), small(HEAD_DIM), small(B_V_DIM),
                  seg(COL_QB), seg(COL_KB), seg(COL_VB), seg(COL_ZB)],
        out_specs=pl.BlockSpec((SEQ, B_V_DIM), lambda b, h: (b, h)),
        scratch_shapes=[pltpu.VMEM((SEQ, 4 * HEAD_DIM), BF16), pltpu.VMEM((SEQ, 4 * HEAD_DIM), BF16)],
        compiler_params=_cparams(("parallel", "parallel")),
        name="diff_branch",
    )(slopes, vec(lq1), vec(lk1), vec(lq2), vec(lk2), vec(subln_g), proj, proj, proj, proj)


SGU_TILE_CHUNKS = 4


def _sgu_kernel(u_ref, v_ref, z_ref, lng_ref, lnb_ref, w_ref, bt_ref, o_ref):
    v = jax.nn.gelu(v_ref[...].astype(F32))
    mu = jnp.mean(v, axis=-1, keepdims=True)
    var = jnp.mean(jnp.square(v - mu), axis=-1, keepdims=True)
    vn = ((v - mu) * lax.rsqrt(var + LN_EPS) * lng_ref[...] + lnb_ref[...]).astype(BF16)
    r_idx = lax.broadcasted_iota(jnp.int32, (SGU_CHUNK, SGU_CHUNK), 0)
    c_idx = lax.broadcasted_iota(jnp.int32, (SGU_CHUNK, SGU_CHUNK), 1)
    causal = r_idx >= c_idx
    chunk_rows = [slice(c * SGU_CHUNK, (c + 1) * SGU_CHUNK) for c in range(SGU_TILE_CHUNKS)]
    for g in range(SGU_GROUPS):
        cols = slice(g * SGU_GROUP_CH, (g + 1) * SGU_GROUP_CH)
        w = jnp.where(causal, w_ref[g], 0.0).astype(BF16)
        vn_g = jnp.concatenate([vn[rows, cols] for rows in chunk_rows], axis=1)
        mixed_g = jnp.dot(w, vn_g, preferred_element_type=F32) + bt_ref[:, g:g + 1]
        for c, rows in enumerate(chunk_rows):
            mixed = mixed_g[:, c * SGU_GROUP_CH:(c + 1) * SGU_GROUP_CH]
            u = jax.nn.gelu(u_ref[rows, cols].astype(F32))
            o_ref[rows, cols] = (u * mixed * jax.nn.silu(z_ref[rows, cols].astype(F32))).astype(o_ref.dtype)


def sgu_branch(proj, ln_g, ln_b, w_s, b_s):
    tile = SGU_TILE_CHUNKS * SGU_CHUNK
    seg = lambda c: pl.BlockSpec((tile, BW), lambda t: (t, c))
    return pl.pallas_call(
        _sgu_kernel,
        out_shape=jax.ShapeDtypeStruct((ROWS, BW), BF16),
        grid=(ROWS // tile,),
        in_specs=[seg(COL_UC), seg(COL_VC), seg(COL_ZC),
                  pl.BlockSpec((1, BW), lambda t: (0, 0)),
                  pl.BlockSpec((1, BW), lambda t: (0, 0)),
                  pl.BlockSpec((SGU_GROUPS, SGU_CHUNK, SGU_CHUNK), lambda t: (0, 0, 0)),
                  pl.BlockSpec((SGU_CHUNK, SGU_GROUPS), lambda t: (0, 0))],
        out_specs=pl.BlockSpec((tile, BW), lambda t: (t, 0)),
        compiler_params=_cparams(("parallel",)),
        name="sgu_branch",
    )(proj, proj, proj, ln_g.reshape(1, BW), ln_b.reshape(1, BW), w_s, b_s.T)


MEM_TQ = 1024


def _mem_kernel(q_ref, kv_ref, z_ref, o_ref):
    c1 = (M_HEAD_DIM ** -0.5) * LOG2E
    for h in range(M_HEADS):
        cols = slice(h * M_HEAD_DIM, (h + 1) * M_HEAD_DIM)
        k = kv_ref[:, cols]
        v = kv_ref[:, BW + h * M_HEAD_DIM:BW + (h + 1) * M_HEAD_DIM]
        for i in range(SEQ // MEM_TQ):
            rows = slice(i * MEM_TQ, (i + 1) * MEM_TQ)
            t = lax.dot_general(q_ref[rows, cols], k, NT_DIMS, preferred_element_type=F32) * c1
            p = jnp.exp2(t - jnp.max(t, axis=-1, keepdims=True))
            l = jnp.sum(p, axis=-1, keepdims=True)
            out = jnp.dot(p.astype(BF16), v, preferred_element_type=F32) / l
            o_ref[rows, cols] = (out * jax.nn.silu(z_ref[rows, cols].astype(F32))).astype(o_ref.dtype)


def mem_branch(proj, mem_kv):
    return pl.pallas_call(
        _mem_kernel,
        out_shape=jax.ShapeDtypeStruct((ROWS, BW), BF16),
        grid=(BATCH,),
        in_specs=[pl.BlockSpec((SEQ, BW), lambda b: (b, COL_QM)),
                  pl.BlockSpec((MEM_LEN, 2 * BW), lambda b: (b, 0)),
                  pl.BlockSpec((SEQ, BW), lambda b: (b, COL_ZM))],
        out_specs=pl.BlockSpec((SEQ, BW), lambda b: (b, 0)),
        compiler_params=_cparams(("parallel",)),
        name="mem_branch",
    )(proj, mem_kv, proj)


MERGE_TM = 512
MERGE_TN = 1024


def _merge_kernel(ya_ref, yb_ref, yc_ref, ym_ref, w_hbm, g0_ref, g1_ref, g2_ref, g3_ref, o_ref,
                  wb_ref, stage_ref, sem, *, layer, n_chunks):
    stream = _WeightStream(w_hbm, wb_ref, stage_ref, sem, layer, o_ref.shape[1])
    w = stream.before_matmul(n_chunks)
    ys = (ya_ref, yb_ref, yc_ref, ym_ref)
    gs = (g0_ref, g1_ref, g2_ref, g3_ref)
    merged = None
    for br in range(N_BRANCH):
        gate = jax.nn.sigmoid(gs[br][...].astype(F32))
        term = gate * jnp.dot(ys[br][...], w[br * BW:(br + 1) * BW, :], preferred_element_type=F32)
        merged = term if merged is None else merged + term
    o_ref[...] = merged.astype(o_ref.dtype)


def gated_merge(ya, yb, yc, ym, w_branch, layer, proj):
    tm, tn = MERGE_TM, MERGE_TN
    n_chunks = ROWS // tm
    k = N_BRANCH * BW
    y_spec = pl.BlockSpec((tm, BW), lambda j, i: (i, 0))
    gate_spec = lambda br: pl.BlockSpec(
        (tm, tn), lambda j, i: (i, (COL_GATE * BW + br * D_MODEL) // tn + j))
    return pl.pallas_call(
        functools.partial(_merge_kernel, layer=layer, n_chunks=n_chunks),
        out_shape=jax.ShapeDtypeStruct((ROWS, D_MODEL), BF16),
        grid=(D_MODEL // tn, n_chunks),
        in_specs=[y_spec, y_spec, y_spec, y_spec,
                  pl.BlockSpec(memory_space=pl.ANY),
                  gate_spec(0), gate_spec(1), gate_spec(2), gate_spec(3)],
        out_specs=pl.BlockSpec((tm, tn), lambda j, i: (i, j)),
        scratch_shapes=_weight_stream_scratch(k, tn, n_chunks),
        compiler_params=_cparams(("arbitrary", "arbitrary")),
        name="gated_merge",
    )(ya, yb, yc, ym, w_branch.reshape(DEPTH, k, D_MODEL), proj, proj, proj, proj)


def kernel(x, mem, norm_g, w_in, mem_norm_g, w_mem_kv, diff_lam_q1, diff_lam_k1, diff_lam_q2, diff_lam_k2,
           diff_subln_g, sgu_ln_g, sgu_ln_b, sgu_w, sgu_b, w_branch, w_out, final_g):
    xf = x.reshape(ROWS, D_MODEL)
    memf = mem.reshape(BATCH * MEM_LEN, D_MODEL)
    slopes = jnp.exp2(-8.0 * jnp.arange(1, N_ALIBI_HEADS + 1, dtype=F32) / N_ALIBI_HEADS)
    for l in range(DEPTH):
        lam_init = 0.8 - 0.6 * math.exp(-0.3 * l)
        h = rmsnorm(xf, norm_g[l], BF16)
        proj = matmul_f32_weights(h, w_in, l, BF16, tm=1024, tn=1024, name="in_proj")
        mem_kv = rmsnorm_matmul_few_rows(memf, mem_norm_g[l], w_mem_kv, l, BF16, name="mem_kv_proj")
        ya = moba_branch(proj, slopes)
        yb = diff_branch(proj, slopes, diff_lam_q1[l], diff_lam_k1[l], diff_lam_q2[l], diff_lam_k2[l],
                         diff_subln_g[l], lam_init)
        yc = sgu_branch(proj, sgu_ln_g[l], sgu_ln_b[l], sgu_w[l], sgu_b[l])
        ym = mem_branch(proj, mem_kv)
        merged = gated_merge(ya, yb, yc, ym, w_branch, l, proj)
        xf = matmul_f32_weights(merged, w_out, l, F32, tm=512, tn=1024, residual=xf, name="out_proj")
    out = rmsnorm(xf, final_g, F32)
    return out.reshape(BATCH, SEQ, D_MODEL)
```

```python
import functools
import math

import jax
import jax.numpy as jnp
import numpy as np
from jax import lax
from jax.experimental import pallas as pl
from jax.experimental.pallas import tpu as pltpu

D_MODEL = 4096
BATCH = 4
SEQ = 2048
DEPTH = 2
MEM_LEN = 256
HEAD_DIM = 128
BW = D_MODEL // 4
N_BRANCH = 4
A_HEADS = BW // HEAD_DIM
MOBA_BLOCK = 256
MOBA_TOPK = 3
B_HEADS = BW // (2 * HEAD_DIM)
B_V_DIM = 2 * HEAD_DIM
SGU_CHUNK = 128
SGU_GROUP_CH = 128
SGU_GROUPS = BW // SGU_GROUP_CH
M_HEADS = 4
M_HEAD_DIM = BW // M_HEADS
N_ALIBI_HEADS = A_HEADS + B_HEADS
RMS_EPS = 1e-6
LN_EPS = 1e-5
NEG = -1e30
N_MOBA_BLOCKS = SEQ // MOBA_BLOCK

(COL_QA, COL_KA, COL_VA, COL_ZA, COL_QB, COL_KB, COL_VB, COL_ZB,
 COL_UC, COL_VC, COL_ZC, COL_QM, COL_ZM, COL_GATE) = range(14)

ROWS = BATCH * SEQ
F32 = jnp.float32
BF16 = jnp.bfloat16
LOG2E = math.log2(math.e)

VMEM_LIMIT_BYTES = 58 * 1024 * 1024

NT_DIMS = (((1,), (1,)), ((), ()))


def _cparams(semantics):
    return pltpu.CompilerParams(dimension_semantics=semantics, vmem_limit_bytes=VMEM_LIMIT_BYTES)


def _as_row_blocks(rows):
    return rows.reshape(rows.shape[0], 1, rows.shape[1])


def _rmsnorm_kernel(x_ref, g_ref, o_ref):
    x = x_ref[...].astype(F32)
    ms = jnp.mean(x * x, axis=-1, keepdims=True)
    o_ref[...] = (x * lax.rsqrt(ms + RMS_EPS) * g_ref[...]).astype(o_ref.dtype)


def rmsnorm(x, g_rows, row, out_dtype, tile_rows=512):
    rows, d = x.shape
    return pl.pallas_call(
        _rmsnorm_kernel,
        out_shape=jax.ShapeDtypeStruct((rows, d), out_dtype),
        grid=(rows // tile_rows,),
        in_specs=[pl.BlockSpec((tile_rows, d), lambda i: (i, 0)),
                  pl.BlockSpec((None, 1, d), lambda i: (row, 0, 0))],
        out_specs=pl.BlockSpec((tile_rows, d), lambda i: (i, 0)),
        compiler_params=_cparams(("parallel",)),
        name="rmsnorm",
    )(x, _as_row_blocks(g_rows))


STAGE_SLOTS = 4


class _WeightStream:
    def __init__(self, w_hbm, wb_ref, stage_ref, sem, layer, tn):
        self.w_hbm, self.wb_ref, self.stage_ref, self.sem = w_hbm, wb_ref, stage_ref, sem
        self.layer, self.tn = layer, tn
        self.j, self.i = pl.program_id(0), pl.program_id(1)
        self.nj, self.ni = pl.num_programs(0), pl.num_programs(1)
        self.chunk_rows = stage_ref.shape[1]

    def _rows(self, chunk):
        return pl.ds(pl.multiple_of(chunk * self.chunk_rows, self.chunk_rows), self.chunk_rows)

    def _copy(self, block, chunk):
        cols = pl.ds(pl.multiple_of(block * self.tn, self.tn), self.tn)
        src = self.w_hbm.at[self.layer, self._rows(chunk), cols]
        slot = chunk % STAGE_SLOTS
        return pltpu.make_async_copy(src, self.stage_ref.at[slot], self.sem.at[slot])

    def _round(self, wb_slot, chunk):
        self.wb_ref[wb_slot, self._rows(chunk), :] = self.stage_ref[chunk % STAGE_SLOTS].astype(BF16)

    def _prefetch_block(self, j):
        return jnp.minimum(j + 1, self.nj - 1)

    def before_matmul(self, n_chunks):
        j, i = self.j, self.i
        step = j * self.ni + i

        @pl.when(step == 0)
        def _():
            depth = STAGE_SLOTS - 1
            for c in range(depth):
                self._copy(0, c).start()
            for c in range(n_chunks):
                if c + depth < n_chunks:
                    self._copy(0, c + depth).start()
                self._copy(0, c).wait()
                self._round(0, c)
            self._copy(self._prefetch_block(0), 0).start()

        self._copy(self._prefetch_block(j), i).wait()

        @pl.when(step + 1 < self.nj * self.ni)
        def _():
            wrap = i + 1 == self.ni
            i_next = jnp.where(wrap, 0, i + 1)
            j_next = jnp.where(wrap, j + 1, j)
            self._copy(self._prefetch_block(j_next), i_next).start()

        self._round((j + 1) % 2, i)
        return self.wb_ref.at[j % 2]


def _mm_wstream_kernel(a_ref, w_hbm, o_ref, wb_ref, stage_ref, sem, *, layer, n_chunks):
    stream = _WeightStream(w_hbm, wb_ref, stage_ref, sem, layer, o_ref.shape[1])
    w = stream.before_matmul(n_chunks)
    o_ref[...] = jnp.dot(a_ref[...], w[...], preferred_element_type=F32).astype(o_ref.dtype)


def _mm_wstream_residual_kernel(a_ref, w_hbm, r_ref, o_ref, wb_ref, stage_ref, sem, *, layer, n_chunks):
    stream = _WeightStream(w_hbm, wb_ref, stage_ref, sem, layer, o_ref.shape[1])
    w = stream.before_matmul(n_chunks)
    acc = jnp.dot(a_ref[...], w[...], preferred_element_type=F32)
    o_ref[...] = (r_ref[...] + acc).astype(o_ref.dtype)


def _weight_stream_scratch(k, tn, n_chunks):
    assert n_chunks % STAGE_SLOTS == 0
    return [pltpu.VMEM((2, k, tn), BF16),
            pltpu.VMEM((STAGE_SLOTS, k // n_chunks, tn), F32),
            pltpu.SemaphoreType.DMA((STAGE_SLOTS,))]


def matmul_f32_weights(a, w_stack, layer, out_dtype, tm, tn, residual=None, name="matmul_w"):
    m, k = a.shape
    n = w_stack.shape[-1]
    n_chunks = m // tm
    in_specs = [pl.BlockSpec((tm, k), lambda j, i: (i, 0)),
                pl.BlockSpec(memory_space=pl.ANY)]
    args = [a, w_stack]
    body = _mm_wstream_kernel
    if residual is not None:
        in_specs.append(pl.BlockSpec((tm, tn), lambda j, i: (i, j)))
        args.append(residual)
        body = _mm_wstream_residual_kernel
    return pl.pallas_call(
        functools.partial(body, layer=layer, n_chunks=n_chunks),
        out_shape=jax.ShapeDtypeStruct((m, n), out_dtype),
        grid=(n // tn, n_chunks),
        in_specs=in_specs,
        out_specs=pl.BlockSpec((tm, tn), lambda j, i: (i, j)),
        scratch_shapes=_weight_stream_scratch(k, tn, n_chunks),
        compiler_params=_cparams(("arbitrary", "arbitrary")),
        name=name,
    )(*args)


SMALL_M_TK = 512


def _rmsnorm_mm_small_m_kernel(x_ref, g_ref, w_ref, o_ref, acc_ref, ss_ref, *, d):
    k = pl.program_id(0)

    @pl.when(k == 0)
    def _():
        acc_ref[...] = jnp.zeros_like(acc_ref)
        ss_ref[...] = jnp.zeros_like(ss_ref)

    x = x_ref[...]
    ss_ref[...] += jnp.sum(x * x, axis=-1, keepdims=True)
    a = (x * g_ref[...]).astype(BF16)
    acc_ref[...] += jnp.dot(a, w_ref[...].astype(BF16), preferred_element_type=F32)

    @pl.when(k == pl.num_programs(0) - 1)
    def _():
        o_ref[...] = (acc_ref[...] * lax.rsqrt(ss_ref[...] / d + RMS_EPS)).astype(o_ref.dtype)


def rmsnorm_matmul_few_rows(x, g_stack, w_stack, layer, out_dtype, name):
    m, k = x.shape
    n = w_stack.shape[-1]
    tk = SMALL_M_TK
    return pl.pallas_call(
        functools.partial(_rmsnorm_mm_small_m_kernel, d=k),
        out_shape=jax.ShapeDtypeStruct((m, n), out_dtype),
        grid=(k // tk,),
        in_specs=[pl.BlockSpec((m, tk), lambda c: (0, c)),
                  pl.BlockSpec((None, 1, tk), lambda c: (layer, 0, c)),
                  pl.BlockSpec((None, tk, n), lambda c: (layer, c, 0))],
        out_specs=pl.BlockSpec((m, n), lambda c: (0, 0)),
        scratch_shapes=[pltpu.VMEM((m, n), F32), pltpu.VMEM((m, 1), F32)],
        compiler_params=_cparams(("arbitrary",)),
        name=name,
    )(x, _as_row_blocks(g_stack), w_stack)


MOBA_TQ = MOBA_BLOCK


ALIBI_TERMS = 3


def _alibi_key_columns(slope, first_col, base):
    pos = lax.broadcasted_iota(jnp.int32, (SEQ, HEAD_DIM), 0) - SEQ // 2
    lane = lax.broadcasted_iota(jnp.int32, (SEQ, HEAD_DIM), 1)
    rest = (slope * HEAD_DIM ** 0.5) * pos.astype(F32)
    cols = base
    for term in range(ALIBI_TERMS):
        piece = rest.astype(BF16).astype(F32)
        rest = rest - piece
        cols = jnp.where(lane == first_col + term, piece, cols)
    return cols


def _alibi_query_columns(first_col, base):
    lane = lax.broadcasted_iota(jnp.int32, base.shape, 1)
    return jnp.where(lane < first_col, base, jnp.where(lane < first_col + ALIBI_TERMS, 1.0, base))


def _moba_kernel(slopes_ref, q_ref, k_ref, v_ref, z_ref, o_ref, qa_ref, ka_ref, va_ref):
    h = pl.program_id(1)
    nb, blk, hd = N_MOBA_BLOCKS, MOBA_BLOCK, HEAD_DIM

    k = k_ref[...]
    ka_ref[:, :hd] = k
    blk_shift = blk.bit_length() - 1
    key_blk = lax.broadcasted_iota(jnp.int32, (SEQ, hd), 0) >> blk_shift
    block_onehot = jnp.where(key_blk == lax.broadcasted_iota(jnp.int32, (SEQ, hd), 1), 1.0, 0.0)
    ka_ref[:, hd:] = _alibi_key_columns(slopes_ref[B_HEADS + h], nb, block_onehot).astype(BF16)
    va_ref[:, :hd] = v_ref[...]
    va_ref[:, hd:] = jnp.ones((SEQ, hd), BF16)

    km = jnp.concatenate(
        [jnp.mean(k[n * blk:(n + 1) * blk, :].astype(F32), axis=0, keepdims=True) for n in range(nb)], axis=0)
    q = q_ref[...]
    gate_t = lax.dot_general(km, q.astype(F32), NT_DIMS, precision=lax.Precision.HIGHEST,
                             preferred_element_type=F32)
    n_idx = lax.broadcasted_iota(jnp.int32, (nb, SEQ), 0)
    q_blk = lax.broadcasted_iota(jnp.int32, (nb, SEQ), 1) >> blk_shift
    past = n_idx < q_blk
    g = jnp.where(past, gate_t, NEG)
    rank = jnp.zeros((nb, SEQ), jnp.int32)
    for m in range(nb):
        row = g[m:m + 1, :]
        rank = rank + jnp.where(row > g, 1, jnp.where(row == g, jnp.where(n_idx > m, 1, 0), 0))
    mask_t = jnp.where(past, jnp.where(rank < MOBA_TOPK, 0.0, NEG),
                       jnp.where(n_idx == q_blk, 0.0, NEG))
    mask_rows = jnp.concatenate([mask_t, jnp.zeros((hd - nb, SEQ), F32)], axis=0).T
    qa_ref[:, :hd] = q
    qa_ref[:, hd:] = _alibi_query_columns(nb, mask_rows).astype(BF16)

    c1 = (hd ** -0.5) * LOG2E
    r_idx = lax.broadcasted_iota(jnp.int32, (MOBA_TQ, blk), 0)
    c_idx = lax.broadcasted_iota(jnp.int32, (MOBA_TQ, blk), 1)
    causal = r_idx >= c_idx

    def raw_scores(i):
        return lax.dot_general(qa_ref[i * MOBA_TQ:(i + 1) * MOBA_TQ, :], ka_ref[:(i + 1) * blk, :], NT_DIMS,
                               preferred_element_type=F32)

    s_next = raw_scores(0)
    for i in range(nb):
        rows = slice(i * MOBA_TQ, (i + 1) * MOBA_TQ)
        width = (i + 1) * blk
        s = s_next
        if i + 1 < nb:
            s_next = raw_scores(i + 1)
        s_own = jnp.where(causal, s[:, i * blk:], NEG)
        s = s_own if i == 0 else jnp.concatenate([s[:, :i * blk], s_own], axis=1)
        p = jnp.exp2((s - jnp.max(s, axis=-1, keepdims=True)) * c1)
        oa = jnp.dot(p.astype(BF16), va_ref[:width, :], preferred_element_type=F32)
        out = oa[:, :hd] / oa[:, hd:]
        o_ref[rows, :] = (out * jax.nn.silu(z_ref[rows, :].astype(F32))).astype(o_ref.dtype)


def moba_branch(proj, slopes):
    hpb = BW // HEAD_DIM
    seg = lambda c: pl.BlockSpec((SEQ, HEAD_DIM), lambda b, h: (b, c * hpb + h))
    return pl.pallas_call(
        _moba_kernel,
        out_shape=jax.ShapeDtypeStruct((ROWS, BW), BF16),
        grid=(BATCH, A_HEADS),
        in_specs=[pl.BlockSpec(memory_space=pltpu.SMEM), seg(COL_QA), seg(COL_KA), seg(COL_VA), seg(COL_ZA)],
        out_specs=pl.BlockSpec((SEQ, HEAD_DIM), lambda b, h: (b, h)),
        scratch_shapes=[pltpu.VMEM((SEQ, 2 * HEAD_DIM), BF16),
                        pltpu.VMEM((SEQ, 2 * HEAD_DIM), BF16),
                        pltpu.VMEM((SEQ, 2 * HEAD_DIM), BF16)],
        compiler_params=_cparams(("parallel", "parallel")),
        name="moba_branch",
    )(slopes, proj, proj, proj, proj)


DIFF_TQ = 256


def _diff_kernel(slopes_ref, lq1_ref, lk1_ref, lq2_ref, lk2_ref, subg_ref, q_ref, k_ref, v_ref, z_ref,
                 o_ref, qa_ref, ka_ref, *, lam_init):
    h = pl.program_id(1)
    tq, hd = DIFF_TQ, HEAD_DIM
    c1 = (hd ** -0.5) * LOG2E
    zeros = jnp.zeros((SEQ, hd), F32)
    q_aug = _alibi_query_columns(0, zeros).astype(BF16)
    k_aug = _alibi_key_columns(slopes_ref[h], 0, zeros).astype(BF16)
    for mp in range(2):
        qa_ref[:, 2 * mp * hd:(2 * mp + 1) * hd] = q_ref[:, mp * hd:(mp + 1) * hd]
        qa_ref[:, (2 * mp + 1) * hd:(2 * mp + 2) * hd] = q_aug
        ka_ref[:, 2 * mp * hd:(2 * mp + 1) * hd] = k_ref[:, mp * hd:(mp + 1) * hd]
        ka_ref[:, (2 * mp + 1) * hd:(2 * mp + 2) * hd] = k_aug
    lam = (jnp.exp(jnp.sum(lq1_ref[...] * lk1_ref[...], axis=-1, keepdims=True))
           - jnp.exp(jnp.sum(lq2_ref[...] * lk2_ref[...], axis=-1, keepdims=True)) + lam_init)
    r_idx = lax.broadcasted_iota(jnp.int32, (tq, tq), 0)
    c_idx = lax.broadcasted_iota(jnp.int32, (tq, tq), 1)
    causal = r_idx >= c_idx
    n_tiles = SEQ // tq

    def raw_scores(i, mp):
        cols = slice(2 * mp * hd, (2 * mp + 2) * hd)
        return lax.dot_general(qa_ref[i * tq:(i + 1) * tq, cols], ka_ref[:(i + 1) * tq, cols], NT_DIMS,
                               preferred_element_type=F32)

    def softmax_numerator(s, i):
        s_own = jnp.where(causal, s[:, i * tq:], NEG)
        s = s_own if i == 0 else jnp.concatenate([s[:, :i * tq], s_own], axis=1)
        p = jnp.exp2((s - jnp.max(s, axis=-1, keepdims=True)) * c1)
        return p.astype(BF16), jnp.sum(p, axis=-1, keepdims=True)

    s0_next = raw_scores(0, 0)
    for i in range(n_tiles):
        rows = slice(i * tq, (i + 1) * tq)
        s0, s1 = s0_next, raw_scores(i, 1)
        p0, l0 = softmax_numerator(s0, i)
        if i + 1 < n_tiles:
            s0_next = raw_scores(i + 1, 0)
        p1, l1 = softmax_numerator(s1, i)
        o = jnp.dot(jnp.concatenate([p0, p1], axis=0), v_ref[:(i + 1) * tq, :], preferred_element_type=F32)
        ob = o[:tq] / l0 - lam * (o[tq:] / l1)
        ms = jnp.mean(ob * ob, axis=-1, keepdims=True)
        ob = ob * lax.rsqrt(ms + RMS_EPS) * subg_ref[...]
        ob = ob * (1.0 - lam_init)
        o_ref[rows, :] = (ob * jax.nn.silu(z_ref[rows, :].astype(F32))).astype(o_ref.dtype)


def diff_branch(proj, slopes, lq1, lk1, lq2, lk2, subln_g, layer, lam_init):
    hpb = BW // B_V_DIM
    small = lambda n: pl.BlockSpec((None, 1, n), lambda b, h: (layer, 0, 0))
    lq1, lk1, lq2, lk2, subln_g = map(_as_row_blocks, (lq1, lk1, lq2, lk2, subln_g))
    seg = lambda c: pl.BlockSpec((SEQ, B_V_DIM), lambda b, h: (b, c * hpb + h))
    return pl.pallas_call(
        functools.partial(_diff_kernel, lam_init=lam_init),
        out_shape=jax.ShapeDtypeStruct((ROWS, BW), BF16),
        grid=(BATCH, B_HEADS),
        in_specs=[pl.BlockSpec(memory_space=pltpu.SMEM),
                  small(HEAD_DIM), small(HEAD_DIM), small(HEAD_DIM), small(HEAD_DIM), small(B_V_DIM),
                  seg(COL_QB), seg(COL_KB), seg(COL_VB), seg(COL_ZB)],
        out_specs=pl.BlockSpec((SEQ, B_V_DIM), lambda b, h: (b, h)),
        scratch_shapes=[pltpu.VMEM((SEQ, 4 * HEAD_DIM), BF16), pltpu.VMEM((SEQ, 4 * HEAD_DIM), BF16)],
        compiler_params=_cparams(("parallel", "parallel")),
        name="diff_branch",
    )(slopes, lq1, lk1, lq2, lk2, subln_g, proj, proj, proj, proj)


SGU_TILE_CHUNKS = 4


def _sgu_kernel(u_ref, v_ref, z_ref, lng_ref, lnb_ref, w_ref, bt_ref, o_ref):
    v = jax.nn.gelu(v_ref[...].astype(F32))
    mu = jnp.mean(v, axis=-1, keepdims=True)
    var = jnp.mean(jnp.square(v - mu), axis=-1, keepdims=True)
    vn = ((v - mu) * lax.rsqrt(var + LN_EPS) * lng_ref[...] + lnb_ref[...]).astype(BF16)
    r_idx = lax.broadcasted_iota(jnp.int32, (SGU_CHUNK, SGU_CHUNK), 0)
    c_idx = lax.broadcasted_iota(jnp.int32, (SGU_CHUNK, SGU_CHUNK), 1)
    causal = r_idx >= c_idx
    chunk_rows = [slice(c * SGU_CHUNK, (c + 1) * SGU_CHUNK) for c in range(SGU_TILE_CHUNKS)]
    for g in range(SGU_GROUPS):
        cols = slice(g * SGU_GROUP_CH, (g + 1) * SGU_GROUP_CH)
        w = jnp.where(causal, w_ref[g], 0.0).astype(BF16)
        vn_g = jnp.concatenate([vn[rows, cols] for rows in chunk_rows], axis=1)
        mixed_g = jnp.dot(w, vn_g, preferred_element_type=F32) + bt_ref[:, g:g + 1]
        for c, rows in enumerate(chunk_rows):
            mixed = mixed_g[:, c * SGU_GROUP_CH:(c + 1) * SGU_GROUP_CH]
            u = jax.nn.gelu(u_ref[rows, cols].astype(F32))
            o_ref[rows, cols] = (u * mixed * jax.nn.silu(z_ref[rows, cols].astype(F32))).astype(o_ref.dtype)


def sgu_branch(proj, ln_g, ln_b, w_s, b_s_t, layer):
    tile = SGU_TILE_CHUNKS * SGU_CHUNK
    seg = lambda c: pl.BlockSpec((tile, BW), lambda t: (t, c))
    return pl.pallas_call(
        _sgu_kernel,
        out_shape=jax.ShapeDtypeStruct((ROWS, BW), BF16),
        grid=(ROWS // tile,),
        in_specs=[seg(COL_UC), seg(COL_VC), seg(COL_ZC),
                  pl.BlockSpec((None, 1, BW), lambda t: (layer, 0, 0)),
                  pl.BlockSpec((None, 1, BW), lambda t: (layer, 0, 0)),
                  pl.BlockSpec((None, SGU_GROUPS, SGU_CHUNK, SGU_CHUNK), lambda t: (layer, 0, 0, 0)),
                  pl.BlockSpec((None, SGU_CHUNK, SGU_GROUPS), lambda t: (layer, 0, 0))],
        out_specs=pl.BlockSpec((tile, BW), lambda t: (t, 0)),
        compiler_params=_cparams(("parallel",)),
        name="sgu_branch",
    )(proj, proj, proj, _as_row_blocks(ln_g), _as_row_blocks(ln_b), w_s, b_s_t)


MEM_TQ = 1024


def _mem_kernel(q_ref, kv_ref, z_ref, o_ref):
    c1 = (M_HEAD_DIM ** -0.5) * LOG2E
    for h in range(M_HEADS):
        cols = slice(h * M_HEAD_DIM, (h + 1) * M_HEAD_DIM)
        k = kv_ref[:, cols]
        v = kv_ref[:, BW + h * M_HEAD_DIM:BW + (h + 1) * M_HEAD_DIM]
        for i in range(SEQ // MEM_TQ):
            rows = slice(i * MEM_TQ, (i + 1) * MEM_TQ)
            t = lax.dot_general(q_ref[rows, cols], k, NT_DIMS, preferred_element_type=F32) * c1
            p = jnp.exp2(t - jnp.max(t, axis=-1, keepdims=True))
            l = jnp.sum(p, axis=-1, keepdims=True)
            out = jnp.dot(p.astype(BF16), v, preferred_element_type=F32) / l
            o_ref[rows, cols] = (out * jax.nn.silu(z_ref[rows, cols].astype(F32))).astype(o_ref.dtype)


def mem_branch(proj, mem_kv):
    return pl.pallas_call(
        _mem_kernel,
        out_shape=jax.ShapeDtypeStruct((ROWS, BW), BF16),
        grid=(BATCH,),
        in_specs=[pl.BlockSpec((SEQ, BW), lambda b: (b, COL_QM)),
                  pl.BlockSpec((MEM_LEN, 2 * BW), lambda b: (b, 0)),
                  pl.BlockSpec((SEQ, BW), lambda b: (b, COL_ZM))],
        out_specs=pl.BlockSpec((SEQ, BW), lambda b: (b, 0)),
        compiler_params=_cparams(("parallel",)),
        name="mem_branch",
    )(proj, mem_kv, proj)


MERGE_TM = 512
MERGE_TN = 1024


def _merge_kernel(ya_ref, yb_ref, yc_ref, ym_ref, w_hbm, g0_ref, g1_ref, g2_ref, g3_ref, o_ref,
                  wb_ref, stage_ref, sem, *, layer, n_chunks):
    stream = _WeightStream(w_hbm, wb_ref, stage_ref, sem, layer, o_ref.shape[1])
    w = stream.before_matmul(n_chunks)
    ys = (ya_ref, yb_ref, yc_ref, ym_ref)
    gs = (g0_ref, g1_ref, g2_ref, g3_ref)
    merged = None
    for br in range(N_BRANCH):
        gate = jax.nn.sigmoid(gs[br][...].astype(F32))
        term = gate * jnp.dot(ys[br][...], w[br * BW:(br + 1) * BW, :], preferred_element_type=F32)
        merged = term if merged is None else merged + term
    o_ref[...] = merged.astype(o_ref.dtype)


def gated_merge(ya, yb, yc, ym, w_branch, layer, proj):
    tm, tn = MERGE_TM, MERGE_TN
    n_chunks = ROWS // tm
    k = N_BRANCH * BW
    y_spec = pl.BlockSpec((tm, BW), lambda j, i: (i, 0))
    gate_spec = lambda br: pl.BlockSpec(
        (tm, tn), lambda j, i: (i, (COL_GATE * BW + br * D_MODEL) // tn + j))
    return pl.pallas_call(
        functools.partial(_merge_kernel, layer=layer, n_chunks=n_chunks),
        out_shape=jax.ShapeDtypeStruct((ROWS, D_MODEL), BF16),
        grid=(D_MODEL // tn, n_chunks),
        in_specs=[y_spec, y_spec, y_spec, y_spec,
                  pl.BlockSpec(memory_space=pl.ANY),
                  gate_spec(0), gate_spec(1), gate_spec(2), gate_spec(3)],
        out_specs=pl.BlockSpec((tm, tn), lambda j, i: (i, j)),
        scratch_shapes=_weight_stream_scratch(k, tn, n_chunks),
        compiler_params=_cparams(("arbitrary", "arbitrary")),
        name="gated_merge",
    )(ya, yb, yc, ym, w_branch.reshape(DEPTH, k, D_MODEL), proj, proj, proj, proj)


def kernel(x, mem, norm_g, w_in, mem_norm_g, w_mem_kv, diff_lam_q1, diff_lam_k1, diff_lam_q2, diff_lam_k2,
           diff_subln_g, sgu_ln_g, sgu_ln_b, sgu_w, sgu_b, w_branch, w_out, final_g):
    xf = x.reshape(ROWS, D_MODEL)
    memf = mem.reshape(BATCH * MEM_LEN, D_MODEL)
    slopes = jnp.asarray(np.exp2(-8.0 * np.arange(1, N_ALIBI_HEADS + 1) / N_ALIBI_HEADS), F32)
    sgu_b_t = jnp.swapaxes(sgu_b, 1, 2)
    for l in range(DEPTH):
        lam_init = 0.8 - 0.6 * math.exp(-0.3 * l)
        h = rmsnorm(xf, norm_g, l, BF16)
        proj = matmul_f32_weights(h, w_in, l, BF16, tm=1024, tn=1024, name="in_proj")
        mem_kv = rmsnorm_matmul_few_rows(memf, mem_norm_g, w_mem_kv, l, BF16, name="mem_kv_proj")
        ya = moba_branch(proj, slopes)
        yb = diff_branch(proj, slopes, diff_lam_q1, diff_lam_k1, diff_lam_q2, diff_lam_k2, diff_subln_g, l,
                         lam_init)
        yc = sgu_branch(proj, sgu_ln_g, sgu_ln_b, sgu_w, sgu_b_t, l)
        ym = mem_branch(proj, mem_kv)
        merged = gated_merge(ya, yb, yc, ym, w_branch, l, proj)
        xf = matmul_f32_weights(merged, w_out, l, F32, tm=512, tn=1024, residual=xf, name="out_proj")
    out = rmsnorm(xf, final_g.reshape(1, D_MODEL), 0, F32)
    return out.reshape(BATCH, SEQ, D_MODEL)
```

```python
import functools
import math

import jax
import jax.numpy as jnp
import numpy as np
from jax import lax
from jax.experimental import pallas as pl
from jax.experimental.pallas import tpu as pltpu

D_MODEL = 4096
BATCH = 4
SEQ = 2048
DEPTH = 2
MEM_LEN = 256
HEAD_DIM = 128
BW = D_MODEL // 4
N_BRANCH = 4
A_HEADS = BW // HEAD_DIM
MOBA_BLOCK = 256
MOBA_TOPK = 3
B_HEADS = BW // (2 * HEAD_DIM)
B_V_DIM = 2 * HEAD_DIM
SGU_CHUNK = 128
SGU_GROUP_CH = 128
SGU_GROUPS = BW // SGU_GROUP_CH
M_HEADS = 4
M_HEAD_DIM = BW // M_HEADS
N_ALIBI_HEADS = A_HEADS + B_HEADS
RMS_EPS = 1e-6
LN_EPS = 1e-5
NEG = -1e30
N_MOBA_BLOCKS = SEQ // MOBA_BLOCK

(COL_QA, COL_KA, COL_VA, COL_ZA, COL_QB, COL_KB, COL_VB, COL_ZB,
 COL_UC, COL_VC, COL_ZC, COL_QM, COL_ZM, COL_GATE) = range(14)

ROWS = BATCH * SEQ
F32 = jnp.float32
BF16 = jnp.bfloat16
LOG2E = math.log2(math.e)

VMEM_LIMIT_BYTES = 58 * 1024 * 1024

NT_DIMS = (((1,), (1,)), ((), ()))


def _cparams(semantics):
    return pltpu.CompilerParams(dimension_semantics=semantics, vmem_limit_bytes=VMEM_LIMIT_BYTES)


def _rmsnorm_kernel(x_ref, g_ref, o_ref, *, row):
    x = x_ref[...].astype(F32)
    ms = jnp.mean(x * x, axis=-1, keepdims=True)
    o_ref[...] = (x * lax.rsqrt(ms + RMS_EPS) * g_ref[row:row + 1, :]).astype(o_ref.dtype)


def rmsnorm(x, g_rows, row, out_dtype, tile_rows=512):
    rows, d = x.shape
    return pl.pallas_call(
        functools.partial(_rmsnorm_kernel, row=row),
        out_shape=jax.ShapeDtypeStruct((rows, d), out_dtype),
        grid=(rows // tile_rows,),
        in_specs=[pl.BlockSpec((tile_rows, d), lambda i: (i, 0)),
                  pl.BlockSpec(g_rows.shape, lambda i: (0, 0))],
        out_specs=pl.BlockSpec((tile_rows, d), lambda i: (i, 0)),
        compiler_params=_cparams(("parallel",)),
        name="rmsnorm",
    )(x, g_rows)


STAGE_SLOTS = 4


class _WeightStream:
    def __init__(self, w_hbm, wb_ref, stage_ref, sem, layer, tn):
        self.w_hbm, self.wb_ref, self.stage_ref, self.sem = w_hbm, wb_ref, stage_ref, sem
        self.layer, self.tn = layer, tn
        self.j, self.i = pl.program_id(0), pl.program_id(1)
        self.nj, self.ni = pl.num_programs(0), pl.num_programs(1)
        self.chunk_rows = stage_ref.shape[1]

    def _rows(self, chunk):
        return pl.ds(pl.multiple_of(chunk * self.chunk_rows, self.chunk_rows), self.chunk_rows)

    def _copy(self, block, chunk):
        cols = pl.ds(pl.multiple_of(block * self.tn, self.tn), self.tn)
        src = self.w_hbm.at[self.layer, self._rows(chunk), cols]
        slot = chunk % STAGE_SLOTS
        return pltpu.make_async_copy(src, self.stage_ref.at[slot], self.sem.at[slot])

    def _round(self, wb_slot, chunk):
        self.wb_ref[wb_slot, self._rows(chunk), :] = self.stage_ref[chunk % STAGE_SLOTS].astype(BF16)

    def _prefetch_block(self, j):
        return jnp.minimum(j + 1, self.nj - 1)

    def before_matmul(self, n_chunks):
        j, i = self.j, self.i
        step = j * self.ni + i

        @pl.when(step == 0)
        def _():
            depth = STAGE_SLOTS - 1
            for c in range(depth):
                self._copy(0, c).start()
            for c in range(n_chunks):
                if c + depth < n_chunks:
                    self._copy(0, c + depth).start()
                self._copy(0, c).wait()
                self._round(0, c)
            self._copy(self._prefetch_block(0), 0).start()

        self._copy(self._prefetch_block(j), i).wait()

        @pl.when(step + 1 < self.nj * self.ni)
        def _():
            wrap = i + 1 == self.ni
            i_next = jnp.where(wrap, 0, i + 1)
            j_next = jnp.where(wrap, j + 1, j)
            self._copy(self._prefetch_block(j_next), i_next).start()

        self._round((j + 1) % 2, i)
        return self.wb_ref.at[j % 2]


def _mm_wstream_kernel(a_ref, w_hbm, o_ref, wb_ref, stage_ref, sem, *, layer, n_chunks):
    stream = _WeightStream(w_hbm, wb_ref, stage_ref, sem, layer, o_ref.shape[1])
    w = stream.before_matmul(n_chunks)
    o_ref[...] = jnp.dot(a_ref[...], w[...], preferred_element_type=F32).astype(o_ref.dtype)


def _mm_wstream_residual_kernel(a_ref, w_hbm, r_ref, o_ref, wb_ref, stage_ref, sem, *, layer, n_chunks):
    stream = _WeightStream(w_hbm, wb_ref, stage_ref, sem, layer, o_ref.shape[1])
    w = stream.before_matmul(n_chunks)
    acc = jnp.dot(a_ref[...], w[...], preferred_element_type=F32)
    o_ref[...] = (r_ref[...] + acc).astype(o_ref.dtype)


def _weight_stream_scratch(k, tn, n_chunks):
    assert n_chunks % STAGE_SLOTS == 0
    return [pltpu.VMEM((2, k, tn), BF16),
            pltpu.VMEM((STAGE_SLOTS, k // n_chunks, tn), F32),
            pltpu.SemaphoreType.DMA((STAGE_SLOTS,))]


def matmul_f32_weights(a, w_stack, layer, out_dtype, tm, tn, residual=None, name="matmul_w"):
    m, k = a.shape
    n = w_stack.shape[-1]
    n_chunks = m // tm
    in_specs = [pl.BlockSpec((tm, k), lambda j, i: (i, 0)),
                pl.BlockSpec(memory_space=pl.ANY)]
    args = [a, w_stack]
    body = _mm_wstream_kernel
    if residual is not None:
        in_specs.append(pl.BlockSpec((tm, tn), lambda j, i: (i, j)))
        args.append(residual)
        body = _mm_wstream_residual_kernel
    return pl.pallas_call(
        functools.partial(body, layer=layer, n_chunks=n_chunks),
        out_shape=jax.ShapeDtypeStruct((m, n), out_dtype),
        grid=(n // tn, n_chunks),
        in_specs=in_specs,
        out_specs=pl.BlockSpec((tm, tn), lambda j, i: (i, j)),
        scratch_shapes=_weight_stream_scratch(k, tn, n_chunks),
        compiler_params=_cparams(("arbitrary", "arbitrary")),
        name=name,
    )(*args)


SMALL_M_TK = 512


def _rmsnorm_mm_small_m_kernel(x_ref, g_ref, w_ref, o_ref, acc_ref, ss_ref, *, d, layer):
    k = pl.program_id(0)

    @pl.when(k == 0)
    def _():
        acc_ref[...] = jnp.zeros_like(acc_ref)
        ss_ref[...] = jnp.zeros_like(ss_ref)

    x = x_ref[...]
    ss_ref[...] += jnp.sum(x * x, axis=-1, keepdims=True)
    a = (x * g_ref[layer:layer + 1, :]).astype(BF16)
    acc_ref[...] += jnp.dot(a, w_ref[...].astype(BF16), preferred_element_type=F32)

    @pl.when(k == pl.num_programs(0) - 1)
    def _():
        o_ref[...] = (acc_ref[...] * lax.rsqrt(ss_ref[...] / d + RMS_EPS)).astype(o_ref.dtype)


def rmsnorm_matmul_few_rows(x, g_stack, w_stack, layer, out_dtype, name):
    m, k = x.shape
    n = w_stack.shape[-1]
    tk = SMALL_M_TK
    return pl.pallas_call(
        functools.partial(_rmsnorm_mm_small_m_kernel, d=k, layer=layer),
        out_shape=jax.ShapeDtypeStruct((m, n), out_dtype),
        grid=(k // tk,),
        in_specs=[pl.BlockSpec((m, tk), lambda c: (0, c)),
                  pl.BlockSpec((g_stack.shape[0], tk), lambda c: (0, c)),
                  pl.BlockSpec((None, tk, n), lambda c: (layer, c, 0))],
        out_specs=pl.BlockSpec((m, n), lambda c: (0, 0)),
        scratch_shapes=[pltpu.VMEM((m, n), F32), pltpu.VMEM((m, 1), F32)],
        compiler_params=_cparams(("arbitrary",)),
        name=name,
    )(x, g_stack, w_stack)


MOBA_TQ = MOBA_BLOCK


ALIBI_TERMS = 3


def _alibi_key_columns(slope, first_col, base):
    pos = lax.broadcasted_iota(jnp.int32, (SEQ, HEAD_DIM), 0) - SEQ // 2
    lane = lax.broadcasted_iota(jnp.int32, (SEQ, HEAD_DIM), 1)
    rest = (slope * HEAD_DIM ** 0.5) * pos.astype(F32)
    cols = base
    for term in range(ALIBI_TERMS):
        piece = rest.astype(BF16).astype(F32)
        rest = rest - piece
        cols = jnp.where(lane == first_col + term, piece, cols)
    return cols


def _alibi_query_columns(first_col, base):
    lane = lax.broadcasted_iota(jnp.int32, base.shape, 1)
    return jnp.where(lane < first_col, base, jnp.where(lane < first_col + ALIBI_TERMS, 1.0, base))


def _moba_kernel(slopes_ref, q_ref, k_ref, v_ref, z_ref, o_ref, qa_ref, ka_ref, va_ref):
    h = pl.program_id(1)
    nb, blk, hd = N_MOBA_BLOCKS, MOBA_BLOCK, HEAD_DIM

    k = k_ref[...]
    ka_ref[:, :hd] = k
    blk_shift = blk.bit_length() - 1
    key_blk = lax.broadcasted_iota(jnp.int32, (SEQ, hd), 0) >> blk_shift
    block_onehot = jnp.where(key_blk == lax.broadcasted_iota(jnp.int32, (SEQ, hd), 1), 1.0, 0.0)
    ka_ref[:, hd:] = _alibi_key_columns(slopes_ref[B_HEADS + h], nb, block_onehot).astype(BF16)
    va_ref[:, :hd] = v_ref[...]
    va_ref[:, hd:] = jnp.ones((SEQ, hd), BF16)

    km = jnp.concatenate(
        [jnp.mean(k[n * blk:(n + 1) * blk, :].astype(F32), axis=0, keepdims=True) for n in range(nb)], axis=0)
    q = q_ref[...]
    gate_t = lax.dot_general(km, q.astype(F32), NT_DIMS, precision=lax.Precision.HIGHEST,
                             preferred_element_type=F32)
    n_idx = lax.broadcasted_iota(jnp.int32, (nb, SEQ), 0)
    q_blk = lax.broadcasted_iota(jnp.int32, (nb, SEQ), 1) >> blk_shift
    past = n_idx < q_blk
    g = jnp.where(past, gate_t, NEG)
    rank = jnp.zeros((nb, SEQ), jnp.int32)
    for m in range(nb):
        row = g[m:m + 1, :]
        rank = rank + jnp.where(row > g, 1, jnp.where(row == g, jnp.where(n_idx > m, 1, 0), 0))
    mask_t = jnp.where(past, jnp.where(rank < MOBA_TOPK, 0.0, NEG),
                       jnp.where(n_idx == q_blk, 0.0, NEG))
    mask_rows = jnp.concatenate([mask_t, jnp.zeros((hd - nb, SEQ), F32)], axis=0).T
    qa_ref[:, :hd] = q
    qa_ref[:, hd:] = _alibi_query_columns(nb, mask_rows).astype(BF16)

    c1 = (hd ** -0.5) * LOG2E
    r_idx = lax.broadcasted_iota(jnp.int32, (MOBA_TQ, blk), 0)
    c_idx = lax.broadcasted_iota(jnp.int32, (MOBA_TQ, blk), 1)
    causal = r_idx >= c_idx

    def raw_scores(i):
        return lax.dot_general(qa_ref[i * MOBA_TQ:(i + 1) * MOBA_TQ, :], ka_ref[:(i + 1) * blk, :], NT_DIMS,
                               preferred_element_type=F32)

    s_next = raw_scores(0)
    for i in range(nb):
        rows = slice(i * MOBA_TQ, (i + 1) * MOBA_TQ)
        width = (i + 1) * blk
        s = s_next
        if i + 1 < nb:
            s_next = raw_scores(i + 1)
        s_own = jnp.where(causal, s[:, i * blk:], NEG)
        s = s_own if i == 0 else jnp.concatenate([s[:, :i * blk], s_own], axis=1)
        p = jnp.exp2((s - jnp.max(s, axis=-1, keepdims=True)) * c1)
        oa = jnp.dot(p.astype(BF16), va_ref[:width, :], preferred_element_type=F32)
        out = oa[:, :hd] / oa[:, hd:]
        o_ref[rows, :] = (out * jax.nn.silu(z_ref[rows, :].astype(F32))).astype(o_ref.dtype)


def moba_branch(proj, slopes):
    hpb = BW // HEAD_DIM
    seg = lambda c: pl.BlockSpec((SEQ, HEAD_DIM), lambda b, h: (b, c * hpb + h))
    return pl.pallas_call(
        _moba_kernel,
        out_shape=jax.ShapeDtypeStruct((ROWS, BW), BF16),
        grid=(BATCH, A_HEADS),
        in_specs=[pl.BlockSpec(memory_space=pltpu.SMEM), seg(COL_QA), seg(COL_KA), seg(COL_VA), seg(COL_ZA)],
        out_specs=pl.BlockSpec((SEQ, HEAD_DIM), lambda b, h: (b, h)),
        scratch_shapes=[pltpu.VMEM((SEQ, 2 * HEAD_DIM), BF16),
                        pltpu.VMEM((SEQ, 2 * HEAD_DIM), BF16),
                        pltpu.VMEM((SEQ, 2 * HEAD_DIM), BF16)],
        compiler_params=_cparams(("parallel", "parallel")),
        name="moba_branch",
    )(slopes, proj, proj, proj, proj)


DIFF_TQ = 256


def _diff_kernel(slopes_ref, lq1_ref, lk1_ref, lq2_ref, lk2_ref, subg_ref, q_ref, k_ref, v_ref, z_ref,
                 o_ref, qa_ref, ka_ref, *, layer, lam_init):
    h = pl.program_id(1)
    tq, hd = DIFF_TQ, HEAD_DIM
    c1 = (hd ** -0.5) * LOG2E
    zeros = jnp.zeros((SEQ, hd), F32)
    q_aug = _alibi_query_columns(0, zeros).astype(BF16)
    k_aug = _alibi_key_columns(slopes_ref[h], 0, zeros).astype(BF16)
    for mp in range(2):
        qa_ref[:, 2 * mp * hd:(2 * mp + 1) * hd] = q_ref[:, mp * hd:(mp + 1) * hd]
        qa_ref[:, (2 * mp + 1) * hd:(2 * mp + 2) * hd] = q_aug
        ka_ref[:, 2 * mp * hd:(2 * mp + 1) * hd] = k_ref[:, mp * hd:(mp + 1) * hd]
        ka_ref[:, (2 * mp + 1) * hd:(2 * mp + 2) * hd] = k_aug
    row = slice(layer, layer + 1)
    lam = (jnp.exp(jnp.sum(lq1_ref[row, :] * lk1_ref[row, :], axis=-1, keepdims=True))
           - jnp.exp(jnp.sum(lq2_ref[row, :] * lk2_ref[row, :], axis=-1, keepdims=True)) + lam_init)
    r_idx = lax.broadcasted_iota(jnp.int32, (tq, tq), 0)
    c_idx = lax.broadcasted_iota(jnp.int32, (tq, tq), 1)
    causal = r_idx >= c_idx
    n_tiles = SEQ // tq

    def raw_scores(i, mp):
        cols = slice(2 * mp * hd, (2 * mp + 2) * hd)
        return lax.dot_general(qa_ref[i * tq:(i + 1) * tq, cols], ka_ref[:(i + 1) * tq, cols], NT_DIMS,
                               preferred_element_type=F32)

    def softmax_numerator(s, i):
        s_own = jnp.where(causal, s[:, i * tq:], NEG)
        s = s_own if i == 0 else jnp.concatenate([s[:, :i * tq], s_own], axis=1)
        p = jnp.exp2((s - jnp.max(s, axis=-1, keepdims=True)) * c1)
        return p.astype(BF16), jnp.sum(p, axis=-1, keepdims=True)

    s0_next = raw_scores(0, 0)
    for i in range(n_tiles):
        rows = slice(i * tq, (i + 1) * tq)
        s0, s1 = s0_next, raw_scores(i, 1)
        p0, l0 = softmax_numerator(s0, i)
        if i + 1 < n_tiles:
            s0_next = raw_scores(i + 1, 0)
        p1, l1 = softmax_numerator(s1, i)
        o = jnp.dot(jnp.concatenate([p0, p1], axis=0), v_ref[:(i + 1) * tq, :], preferred_element_type=F32)
        ob = o[:tq] / l0 - lam * (o[tq:] / l1)
        ms = jnp.mean(ob * ob, axis=-1, keepdims=True)
        ob = ob * lax.rsqrt(ms + RMS_EPS) * subg_ref[row, :]
        ob = ob * (1.0 - lam_init)
        o_ref[rows, :] = (ob * jax.nn.silu(z_ref[rows, :].astype(F32))).astype(o_ref.dtype)


def diff_branch(proj, slopes, lq1, lk1, lq2, lk2, subln_g, layer, lam_init):
    hpb = BW // B_V_DIM
    small = lambda n: pl.BlockSpec((DEPTH, n), lambda b, h: (0, 0))
    seg = lambda c: pl.BlockSpec((SEQ, B_V_DIM), lambda b, h: (b, c * hpb + h))
    return pl.pallas_call(
        functools.partial(_diff_kernel, layer=layer, lam_init=lam_init),
        out_shape=jax.ShapeDtypeStruct((ROWS, BW), BF16),
        grid=(BATCH, B_HEADS),
        in_specs=[pl.BlockSpec(memory_space=pltpu.SMEM),
                  small(HEAD_DIM), small(HEAD_DIM), small(HEAD_DIM), small(HEAD_DIM), small(B_V_DIM),
                  seg(COL_QB), seg(COL_KB), seg(COL_VB), seg(COL_ZB)],
        out_specs=pl.BlockSpec((SEQ, B_V_DIM), lambda b, h: (b, h)),
        scratch_shapes=[pltpu.VMEM((SEQ, 4 * HEAD_DIM), BF16), pltpu.VMEM((SEQ, 4 * HEAD_DIM), BF16)],
        compiler_params=_cparams(("parallel", "parallel")),
        name="diff_branch",
    )(slopes, lq1, lk1, lq2, lk2, subln_g, proj, proj, proj, proj)


SGU_TILE_CHUNKS = 4


def _sgu_kernel(u_ref, v_ref, z_ref, lng_ref, lnb_ref, w_ref, bt_ref, o_ref, *, layer):
    v = jax.nn.gelu(v_ref[...].astype(F32))
    mu = jnp.mean(v, axis=-1, keepdims=True)
    var = jnp.mean(jnp.square(v - mu), axis=-1, keepdims=True)
    row = slice(layer, layer + 1)
    vn = ((v - mu) * lax.rsqrt(var + LN_EPS) * lng_ref[row, :] + lnb_ref[row, :]).astype(BF16)
    r_idx = lax.broadcasted_iota(jnp.int32, (SGU_CHUNK, SGU_CHUNK), 0)
    c_idx = lax.broadcasted_iota(jnp.int32, (SGU_CHUNK, SGU_CHUNK), 1)
    causal = r_idx >= c_idx
    chunk_rows = [slice(c * SGU_CHUNK, (c + 1) * SGU_CHUNK) for c in range(SGU_TILE_CHUNKS)]
    for g in range(SGU_GROUPS):
        cols = slice(g * SGU_GROUP_CH, (g + 1) * SGU_GROUP_CH)
        w = jnp.where(causal, w_ref[g], 0.0).astype(BF16)
        vn_g = jnp.concatenate([vn[rows, cols] for rows in chunk_rows], axis=1)
        mixed_g = jnp.dot(w, vn_g, preferred_element_type=F32) + bt_ref[:, g:g + 1]
        for c, rows in enumerate(chunk_rows):
            mixed = mixed_g[:, c * SGU_GROUP_CH:(c + 1) * SGU_GROUP_CH]
            u = jax.nn.gelu(u_ref[rows, cols].astype(F32))
            o_ref[rows, cols] = (u * mixed * jax.nn.silu(z_ref[rows, cols].astype(F32))).astype(o_ref.dtype)


def sgu_branch(proj, ln_g, ln_b, w_s, b_s_t, layer):
    tile = SGU_TILE_CHUNKS * SGU_CHUNK
    seg = lambda c: pl.BlockSpec((tile, BW), lambda t: (t, c))
    return pl.pallas_call(
        functools.partial(_sgu_kernel, layer=layer),
        out_shape=jax.ShapeDtypeStruct((ROWS, BW), BF16),
        grid=(ROWS // tile,),
        in_specs=[seg(COL_UC), seg(COL_VC), seg(COL_ZC),
                  pl.BlockSpec((DEPTH, BW), lambda t: (0, 0)),
                  pl.BlockSpec((DEPTH, BW), lambda t: (0, 0)),
                  pl.BlockSpec((None, SGU_GROUPS, SGU_CHUNK, SGU_CHUNK), lambda t: (layer, 0, 0, 0)),
                  pl.BlockSpec((None, SGU_CHUNK, SGU_GROUPS), lambda t: (layer, 0, 0))],
        out_specs=pl.BlockSpec((tile, BW), lambda t: (t, 0)),
        compiler_params=_cparams(("parallel",)),
        name="sgu_branch",
    )(proj, proj, proj, ln_g, ln_b, w_s, b_s_t)


MEM_TQ = 1024


def _mem_kernel(q_ref, kv_ref, z_ref, o_ref):
    c1 = (M_HEAD_DIM ** -0.5) * LOG2E
    for h in range(M_HEADS):
        cols = slice(h * M_HEAD_DIM, (h + 1) * M_HEAD_DIM)
        k = kv_ref[:, cols]
        v = kv_ref[:, BW + h * M_HEAD_DIM:BW + (h + 1) * M_HEAD_DIM]
        for i in range(SEQ // MEM_TQ):
            rows = slice(i * MEM_TQ, (i + 1) * MEM_TQ)
            t = lax.dot_general(q_ref[rows, cols], k, NT_DIMS, preferred_element_type=F32) * c1
            p = jnp.exp2(t - jnp.max(t, axis=-1, keepdims=True))
            l = jnp.sum(p, axis=-1, keepdims=True)
            out = jnp.dot(p.astype(BF16), v, preferred_element_type=F32) / l
            o_ref[rows, cols] = (out * jax.nn.silu(z_ref[rows, cols].astype(F32))).astype(o_ref.dtype)


def mem_branch(proj, mem_kv):
    return pl.pallas_call(
        _mem_kernel,
        out_shape=jax.ShapeDtypeStruct((ROWS, BW), BF16),
        grid=(BATCH,),
        in_specs=[pl.BlockSpec((SEQ, BW), lambda b: (b, COL_QM)),
                  pl.BlockSpec((MEM_LEN, 2 * BW), lambda b: (b, 0)),
                  pl.BlockSpec((SEQ, BW), lambda b: (b, COL_ZM))],
        out_specs=pl.BlockSpec((SEQ, BW), lambda b: (b, 0)),
        compiler_params=_cparams(("parallel",)),
        name="mem_branch",
    )(proj, mem_kv, proj)


MERGE_TM = 512
MERGE_TN = 1024


def _merge_kernel(ya_ref, yb_ref, yc_ref, ym_ref, w_hbm, g0_ref, g1_ref, g2_ref, g3_ref, o_ref,
                  wb_ref, stage_ref, sem, *, layer, n_chunks):
    stream = _WeightStream(w_hbm, wb_ref, stage_ref, sem, layer, o_ref.shape[1])
    w = stream.before_matmul(n_chunks)
    ys = (ya_ref, yb_ref, yc_ref, ym_ref)
    gs = (g0_ref, g1_ref, g2_ref, g3_ref)
    merged = None
    for br in range(N_BRANCH):
        gate = jax.nn.sigmoid(gs[br][...].astype(F32))
        term = gate * jnp.dot(ys[br][...], w[br * BW:(br + 1) * BW, :], preferred_element_type=F32)
        merged = term if merged is None else merged + term
    o_ref[...] = merged.astype(o_ref.dtype)


def gated_merge(ya, yb, yc, ym, w_branch, layer, proj):
    tm, tn = MERGE_TM, MERGE_TN
    n_chunks = ROWS // tm
    k = N_BRANCH * BW
    y_spec = pl.BlockSpec((tm, BW), lambda j, i: (i, 0))
    gate_spec = lambda br: pl.BlockSpec(
        (tm, tn), lambda j, i: (i, (COL_GATE * BW + br * D_MODEL) // tn + j))
    return pl.pallas_call(
        functools.partial(_merge_kernel, layer=layer, n_chunks=n_chunks),
        out_shape=jax.ShapeDtypeStruct((ROWS, D_MODEL), BF16),
        grid=(D_MODEL // tn, n_chunks),
        in_specs=[y_spec, y_spec, y_spec, y_spec,
                  pl.BlockSpec(memory_space=pl.ANY),
                  gate_spec(0), gate_spec(1), gate_spec(2), gate_spec(3)],
        out_specs=pl.BlockSpec((tm, tn), lambda j, i: (i, j)),
        scratch_shapes=_weight_stream_scratch(k, tn, n_chunks),
        compiler_params=_cparams(("arbitrary", "arbitrary")),
        name="gated_merge",
    )(ya, yb, yc, ym, w_branch.reshape(DEPTH, k, D_MODEL), proj, proj, proj, proj)


def kernel(x, mem, norm_g, w_in, mem_norm_g, w_mem_kv, diff_lam_q1, diff_lam_k1, diff_lam_q2, diff_lam_k2,
           diff_subln_g, sgu_ln_g, sgu_ln_b, sgu_w, sgu_b, w_branch, w_out, final_g):
    xf = x.reshape(ROWS, D_MODEL)
    memf = mem.reshape(BATCH * MEM_LEN, D_MODEL)
    slopes = jnp.asarray(np.exp2(-8.0 * np.arange(1, N_ALIBI_HEADS + 1) / N_ALIBI_HEADS), F32)
    sgu_b_t = jnp.swapaxes(sgu_b, 1, 2)
    for l in range(DEPTH):
        lam_init = 0.8 - 0.6 * math.exp(-0.3 * l)
        h = rmsnorm(xf, norm_g, l, BF16)
        proj = matmul_f32_weights(h, w_in, l, BF16, tm=1024, tn=1024, name="in_proj")
        mem_kv = rmsnorm_matmul_few_rows(memf, mem_norm_g, w_mem_kv, l, BF16, name="mem_kv_proj")
        ya = moba_branch(proj, slopes)
        yb = diff_branch(proj, slopes, diff_lam_q1, diff_lam_k1, diff_lam_q2, diff_lam_k2, diff_subln_g, l,
                         lam_init)
        yc = sgu_branch(proj, sgu_ln_g, sgu_ln_b, sgu_w, sgu_b_t, l)
        ym = mem_branch(proj, mem_kv)
        merged = gated_merge(ya, yb, yc, ym, w_branch, l, proj)
        xf = matmul_f32_weights(merged, w_out, l, F32, tm=512, tn=1024, residual=xf, name="out_proj")
    out = rmsnorm(xf, final_g.reshape(1, D_MODEL), 0, F32)
    return out.reshape(BATCH, SEQ, D_MODEL)
```
